```python
import jax, jax.numpy as jnp
from jax import lax
import numpy as np

D_MODEL = 2048
BATCH = 2
SEQ = 8192
DEPTH = 4

N_MIXERS = 3
EPS = 1e-6
CONV_WIDTH = 31
MLSTM_HEADS = 8
MLSTM_INNER = 2 * D_MODEL
MLSTM_DV = MLSTM_INNER // MLSTM_HEADS
MLSTM_DQK = MLSTM_DV // 2
MLSTM_HQK = MLSTM_HEADS * MLSTM_DQK
MLSTM_HV = MLSTM_HEADS * MLSTM_DV
MLSTM_PROJ = 2 * MLSTM_HQK + 2 * MLSTM_HV + 2 * MLSTM_HEADS
MLSTM_CHUNK = 64
QK_CONV_WIDTH = 4
SGU_CHUNK = 128
SGU_GROUPS = 8
SGU_WIDTH = D_MODEL
D_FF = 4 * D_MODEL
N_A = (DEPTH + 2) // 3
N_B = (DEPTH + 1) // 3
N_C = DEPTH // 3

kernel_name = "hybrid_conv_mlstm_sgu_trunk"


def rmsnorm(x, g):
    xf = x.astype(jnp.float32)
    y = xf * lax.rsqrt(jnp.mean(xf * xf, axis=-1, keepdims=True) + EPS)
    return (y * g.astype(jnp.float32)).astype(x.dtype)


def layernorm(x, g, b):
    xf = x.astype(jnp.float32)
    mu = jnp.mean(xf, axis=-1, keepdims=True)
    var = jnp.mean(jnp.square(xf - mu), axis=-1, keepdims=True)
    y = (xf - mu) * lax.rsqrt(var + EPS)
    return (y * g.astype(jnp.float32) + b.astype(jnp.float32)).astype(x.dtype)


def causal_depthwise_conv(x, w, bias):
    K = w.shape[0]
    y = lax.conv_general_dilated(
        x, w[:, None, :], window_strides=(1,), padding=[(K - 1, 0)],
        dimension_numbers=('NWC', 'WIO', 'NWC'), feature_group_count=x.shape[-1])
    return y + bias


def conformer_mixer(h, w_in, b_in, w_dw, b_dw, ln_g, ln_b, w_out, b_out):
    a = h @ w_in + b_in
    val, gate = jnp.split(a, 2, axis=-1)
    g = val * jax.nn.sigmoid(gate)
    g = causal_depthwise_conv(g, w_dw, b_dw)
    g = jax.nn.silu(layernorm(g, ln_g, ln_b))
    return g @ w_out + b_out


def mlstm_chunkwise(q, k, v, ig, fg):
    B_, S_, H, DQK = q.shape
    DV = v.shape[-1]
    L = MLSTM_CHUNK
    NC = S_ // L

    def to_chunks(t):
        t = t.astype(jnp.float32).reshape(B_, NC, L, H, -1)
        return jnp.transpose(t, (1, 0, 3, 2, 4))

    qc = to_chunks(q) * (DQK ** -0.5)
    kc = to_chunks(k)
    vc = to_chunks(v)
    ic = to_chunks(ig[..., None])[..., 0]
    lfc = to_chunks(jax.nn.log_sigmoid(fg.astype(jnp.float32))[..., None])[..., 0]
    causal = jnp.tril(jnp.ones((L, L), dtype=bool))

    def step(carry, inp):
        C, n, m = carry
        q_, k_, v_, i_, lf_ = inp
        b = jnp.cumsum(lf_, axis=-1)
        logD = b[..., :, None] - b[..., None, :] + i_[..., None, :]
        logD = jnp.where(causal, logD, -jnp.inf)
        log_inter = b + m[..., None]
        m_t = jnp.maximum(log_inter, jnp.max(logD, axis=-1))
        Dmat = jnp.exp(logD - m_t[..., None])
        inter = jnp.exp(log_inter - m_t)
        s = jnp.einsum('bhtd,bhsd->bhts', q_, k_) * Dmat
        num = jnp.einsum('bhts,bhsv->bhtv', s, v_) + inter[..., None] * jnp.einsum('bhtd,bhdv->bhtv', q_, C)
        den = jnp.sum(s, axis=-1) + inter * jnp.einsum('bhtd,bhd->bht', q_, n)
        h_ = num / jnp.maximum(jnp.abs(den), jnp.exp(-m_t))[..., None]
        m_new = m_t[..., -1]
        w_src = jnp.exp(b[..., -1:] - b + i_ - m_new[..., None])
        decay = jnp.exp(b[..., -1] + m - m_new)
        C_new = decay[..., None, None] * C + jnp.einsum('bhs,bhsd,bhsv->bhdv', w_src, k_, v_)
        n_new = decay[..., None] * n + jnp.einsum('bhs,bhsd->bhd', w_src, k_)
        return (C_new, n_new, m_new), h_

    init = (jnp.zeros((B_, H, DQK, DV), jnp.float32),
            jnp.zeros((B_, H, DQK), jnp.float32),
            jnp.zeros((B_, H), jnp.float32))
    _, hs = lax.scan(step, init, (qc, kc, vc, ic, lfc))
    return jnp.transpose(hs, (1, 0, 3, 2, 4)).reshape(B_, S_, H, DV)


def mlstm_mixer(h, w_in, b_in, w_qkconv, b_qkconv, norm_g, w_out):
    B_, S_, _ = h.shape
    p = h @ w_in + b_in
    o1 = 2 * MLSTM_HQK
    o2 = o1 + MLSTM_HV
    o3 = o2 + MLSTM_HV
    o4 = o3 + MLSTM_HEADS
    qk, v, o, ig, fg = p[..., :o1], p[..., o1:o2], p[..., o2:o3], p[..., o3:o4], p[..., o4:]
    qk = jax.nn.silu(causal_depthwise_conv(qk, w_qkconv, b_qkconv))
    q, k = jnp.split(qk, 2, axis=-1)
    q = q.reshape(B_, S_, MLSTM_HEADS, MLSTM_DQK)
    k = k.reshape(B_, S_, MLSTM_HEADS, MLSTM_DQK)
    v = v.reshape(B_, S_, MLSTM_HEADS, MLSTM_DV)
    ht = mlstm_chunkwise(q, k, v, ig, fg)
    ht = ht * lax.rsqrt(jnp.mean(ht * ht, axis=-1, keepdims=True) + EPS)
    ht = (ht * norm_g.astype(jnp.float32).reshape(MLSTM_HEADS, MLSTM_DV)).reshape(B_, S_, MLSTM_HV)
    y = jax.nn.sigmoid(o) * ht.astype(h.dtype)
    return y @ w_out


def sgu_mixer(h, w_in, b_in, norm_g, w_s, b_s, w_out, b_out):
    B_, S_, _ = h.shape
    p = jax.nn.gelu(h @ w_in + b_in, approximate=False)
    u, v = jnp.split(p, 2, axis=-1)
    v = rmsnorm(v, norm_g)
    n_chunks = S_ // SGU_CHUNK
    vc = v.reshape(B_, n_chunks, SGU_CHUNK, SGU_GROUPS, SGU_WIDTH // SGU_GROUPS)
    ws = w_s * jnp.tril(jnp.ones((SGU_CHUNK, SGU_CHUNK), w_s.dtype))
    mixed = jnp.einsum('gts,bcsgd->bctgd', ws, vc) + jnp.transpose(b_s)[None, None, :, :, None]
    gated = u * mixed.reshape(B_, S_, SGU_WIDTH)
    return gated @ w_out + b_out


def sq_relu_mlp(h, w1, w2):
    return jnp.square(jax.nn.relu(h @ w1)) @ w2


def setup_inputs(seed: int = 0) -> dict:
    key = jax.random.key(seed)
    ks = iter(jax.random.split(key, 40))
    f32 = jnp.float32

    def nrm(shape, scale):
        return jax.random.normal(next(ks), shape, f32) * scale

    def gain(shape):
        return 1.0 + nrm(shape, 0.02)

    D = D_MODEL
    x = jax.random.normal(next(ks), (BATCH, SEQ, D), f32)
    mlstm_b_in = nrm((N_B, MLSTM_PROJ), 0.02).at[:, -MLSTM_HEADS:].add(3.0)
    return {
        "x": x,
        "norm_mix_g": gain((DEPTH, D)),
        "norm_ffn_g": gain((DEPTH, D)),
        "final_g": gain((D,)),
        "conv_w_in": nrm((N_A, D, 2 * D), D ** -0.5),
        "conv_b_in": nrm((N_A, 2 * D), 0.02),
        "conv_w_dw": nrm((N_A, CONV_WIDTH, D), CONV_WIDTH ** -0.5),
        "conv_b_dw": nrm((N_A, D), 0.02),
        "conv_ln_g": gain((N_A, D)),
        "conv_ln_b": nrm((N_A, D), 0.02),
        "conv_w_out": nrm((N_A, D, D), D ** -0.5),
        "conv_b_out": nrm((N_A, D), 0.02),
        "mlstm_w_in": nrm((N_B, D, MLSTM_PROJ), D ** -0.5),
        "mlstm_b_in": mlstm_b_in,
        "mlstm_w_qkconv": nrm((N_B, QK_CONV_WIDTH, 2 * MLSTM_HQK), QK_CONV_WIDTH ** -0.5),
        "mlstm_b_qkconv": nrm((N_B, 2 * MLSTM_HQK), 0.02),
        "mlstm_norm_g": gain((N_B, MLSTM_HV)),
        "mlstm_w_out": nrm((N_B, MLSTM_HV, D), MLSTM_HV ** -0.5),
        "sgu_w_in": nrm((N_C, D, 2 * SGU_WIDTH), D ** -0.5),
        "sgu_b_in": nrm((N_C, 2 * SGU_WIDTH), 0.02),
        "sgu_norm_g": gain((N_C, SGU_WIDTH)),
        "sgu_w_s": nrm((N_C, SGU_GROUPS, SGU_CHUNK, SGU_CHUNK), SGU_CHUNK ** -0.5),
        "sgu_b_s": 1.0 + nrm((N_C, SGU_GROUPS, SGU_CHUNK), 0.1),
        "sgu_w_out": nrm((N_C, SGU_WIDTH, D), SGU_WIDTH ** -0.5),
        "sgu_b_out": nrm((N_C, D), 0.02),
        "ffn_w1": nrm((DEPTH, D, D_FF), D ** -0.5),
        "ffn_w2": nrm((DEPTH, D_FF, D), D_FF ** -0.5),
    }


def reference(x, norm_mix_g, norm_ffn_g, final_g,
              conv_w_in, conv_b_in, conv_w_dw, conv_b_dw, conv_ln_g, conv_ln_b, conv_w_out, conv_b_out,
              mlstm_w_in, mlstm_b_in, mlstm_w_qkconv, mlstm_b_qkconv, mlstm_norm_g, mlstm_w_out,
              sgu_w_in, sgu_b_in, sgu_norm_g, sgu_w_s, sgu_b_s, sgu_w_out, sgu_b_out,
              ffn_w1, ffn_w2):
    for i in range(DEPTH):
        kind = i % N_MIXERS
        j = i // N_MIXERS
        h = rmsnorm(x, norm_mix_g[i])
        if kind == 0:
            y = conformer_mixer(h, conv_w_in[j], conv_b_in[j], conv_w_dw[j], conv_b_dw[j],
                                conv_ln_g[j], conv_ln_b[j], conv_w_out[j], conv_b_out[j])
        elif kind == 1:
            y = mlstm_mixer(h, mlstm_w_in[j], mlstm_b_in[j], mlstm_w_qkconv[j], mlstm_b_qkconv[j],
                            mlstm_norm_g[j], mlstm_w_out[j])
        else:
            y = sgu_mixer(h, sgu_w_in[j], sgu_b_in[j], sgu_norm_g[j], sgu_w_s[j], sgu_b_s[j],
                          sgu_w_out[j], sgu_b_out[j])
        x = x + y
        x = x + sq_relu_mlp(rmsnorm(x, norm_ffn_g[i]), ffn_w1[i], ffn_w2[i])
    return rmsnorm(x, final_g)
```

```python
import functools

import jax
import jax.numpy as jnp
from jax import lax
from jax.experimental import pallas as pl
from jax.experimental.pallas import tpu as pltpu

EPS = 1e-6
N_MIXERS = 3
MLSTM_HEADS = 8
MLSTM_CHUNK = 256
QK_HALO = 8
SGU_CHUNK = 128
SGU_GROUPS = 8
CONV_HALO = 32

F32 = jnp.float32
BF16 = jnp.bfloat16

V7X_VMEM_BYTES = 64 * 1024 * 1024
VMEM_LIMIT = V7X_VMEM_BYTES - 8 * 1024 * 1024


def _params(*sem):
    return pltpu.CompilerParams(dimension_semantics=sem, vmem_limit_bytes=VMEM_LIMIT)


def _rmsnorm_rows(x_ref, g_ref, h_ref, rows=256):
    g = g_ref[...]

    def body(i, carry):
        r = pl.ds(pl.multiple_of(i * rows, rows), rows)
        x = x_ref[r, :]
        ms = jnp.mean(x * x, axis=-1, keepdims=True)
        h_ref[r, :] = (x * lax.rsqrt(ms + EPS) * g).astype(h_ref.dtype)
        return carry

    lax.fori_loop(0, x_ref.shape[0] // rows, body, 0)


def _norm_mm_body(*refs, n_w, epilogue):
    x_ref, g_ref = refs[0], refs[1]
    w_refs = refs[2:2 + n_w]
    b_refs = refs[2 + n_w:2 + 2 * n_w]
    o_ref, h_ref = refs[2 + 2 * n_w], refs[3 + 2 * n_w]

    @pl.when(pl.program_id(1) == 0)
    def _():
        _rmsnorm_rows(x_ref, g_ref, h_ref)

    h = h_ref[...]
    accs = [jnp.dot(h, w[...], preferred_element_type=F32) + b[...]
            for w, b in zip(w_refs, b_refs)]
    o_ref[...] = epilogue(*accs).astype(o_ref.dtype)


def _norm_matmul(x, g, w, b, col_offsets, n_out, epilogue, tm, tn, out_dtype=F32):
    m, k = x.shape
    n_w = len(col_offsets)
    w_specs = [pl.BlockSpec((k, tn), functools.partial(lambda i, j, o: (0, j + o), o=off // tn))
               for off in col_offsets]
    b_specs = [pl.BlockSpec((1, tn), functools.partial(lambda i, j, o: (0, j + o), o=off // tn))
               for off in col_offsets]
    return pl.pallas_call(
        functools.partial(_norm_mm_body, n_w=n_w, epilogue=epilogue),
        name="norm_matmul_" + epilogue.__name__.strip("_"),
        grid=(m // tm, n_out // tn),
        in_specs=[pl.BlockSpec((tm, k), lambda i, j: (i, 0)),
                  pl.BlockSpec((1, k), lambda i, j: (0, 0))] + w_specs + b_specs,
        out_specs=pl.BlockSpec((tm, tn), lambda i, j: (i, j)),
        out_shape=jax.ShapeDtypeStruct((m, n_out), out_dtype),
        scratch_shapes=[pltpu.VMEM((tm, k), BF16)],
        compiler_params=_params("parallel", "arbitrary"),
    )(x, g.reshape(1, k), *([w] * n_w), *([b.reshape(1, -1)] * n_w))


def _mm_res_body(a_ref, w_ref, x_ref, o_ref):
    o_ref[...] = x_ref[...] + jnp.dot(a_ref[...], w_ref[...], preferred_element_type=F32)


def _matmul_residual(a, w, x, tm, tn):
    m, k = a.shape
    n = w.shape[1]
    return pl.pallas_call(
        _mm_res_body,
        name="matmul_residual",
        grid=(m // tm, n // tn),
        in_specs=[pl.BlockSpec((tm, k), lambda i, j: (i, 0)),
                  pl.BlockSpec((k, tn), lambda i, j: (0, j)),
                  pl.BlockSpec((tm, tn), lambda i, j: (i, j))],
        out_specs=pl.BlockSpec((tm, tn), lambda i, j: (i, j)),
        out_shape=jax.ShapeDtypeStruct((m, n), F32),
        compiler_params=_params("parallel", "arbitrary"),
    )(a, w, x)


def _ffn_body(x_ref, g_ref, w1_ref, w2_ref, fg_ref, o_ref, h_ref, *, final_norm):
    j = pl.program_id(1)

    @pl.when(j == 0)
    def _():
        _rmsnorm_rows(x_ref, g_ref, h_ref)
        o_ref[...] = x_ref[...]

    a = jnp.dot(h_ref[...], w1_ref[...], preferred_element_type=F32)
    a = jnp.square(jnp.maximum(a, 0.0)).astype(BF16)
    o_ref[...] += jnp.dot(a, w2_ref[...], preferred_element_type=F32)

    if final_norm:
        @pl.when(j == pl.num_programs(1) - 1)
        def _():
            _rmsnorm_rows(o_ref, fg_ref, o_ref)


def _ffn(x, g, w1, w2, final_g, final_norm, tm, tf):
    m, d = x.shape
    f = w1.shape[1]
    return pl.pallas_call(
        functools.partial(_ffn_body, final_norm=final_norm),
        name="ffn_final" if final_norm else "ffn",
        grid=(m // tm, f // tf),
        in_specs=[pl.BlockSpec((tm, d), lambda i, j: (i, 0)),
                  pl.BlockSpec((1, d), lambda i, j: (0, 0)),
                  pl.BlockSpec((d, tf), lambda i, j: (0, j)),
                  pl.BlockSpec((tf, d), lambda i, j: (j, 0)),
                  pl.BlockSpec((1, d), lambda i, j: (0, 0))],
        out_specs=pl.BlockSpec((tm, d), lambda i, j: (i, 0)),
        out_shape=jax.ShapeDtypeStruct((m, d), F32),
        scratch_shapes=[pltpu.VMEM((tm, d), BF16)],
        compiler_params=_params("parallel", "arbitrary"),
    )(x, g.reshape(1, d), w1, w2, final_g.reshape(1, d))


def _conv_body(gc_ref, gp_ref, x_ref, wdw_ref, bdw_ref, lng_ref, lnb_ref, wout_ref, bout_ref,
               o_ref, ext_ref, y_ref, s_ref, *, seq_tiles, row_chunk, lane_chunk):
    ts, d = gc_ref.shape
    kw = wdw_ref.shape[0]
    first = (pl.program_id(0) % seq_tiles) == 0
    ext_ref[0:CONV_HALO, :] = jnp.where(first, 0.0, gp_ref[...])
    ext_ref[CONV_HALO:, :] = gc_ref[...]
    lead = CONV_HALO - (kw - 1)

    for lc in range(d // lane_chunk):
        lanes = slice(lc * lane_chunk, (lc + 1) * lane_chunk)

        def row_body(rc, carry, lanes=lanes):
            base = pl.multiple_of(rc * row_chunk, row_chunk)
            acc = jnp.broadcast_to(bdw_ref[:, lanes], (row_chunk, lane_chunk))
            win = ext_ref.at[pl.ds(base, row_chunk + CONV_HALO), lanes]
            for k in range(kw):
                acc = acc + wdw_ref[k:k + 1, lanes] * win[lead + k:lead + k + row_chunk, :]
            y_ref[pl.ds(base, row_chunk), lanes] = acc
            return carry

        lax.fori_loop(0, ts // row_chunk, row_body, 0)

    lng = lng_ref[...]
    lnb = lnb_ref[...]
    rows = 256

    def ln_body(i, carry):
        r = pl.ds(pl.multiple_of(i * rows, rows), rows)
        y = y_ref[r, :]
        mu = jnp.mean(y, axis=-1, keepdims=True)
        yc = y - mu
        var = jnp.mean(yc * yc, axis=-1, keepdims=True)
        yn = yc * lax.rsqrt(var + EPS) * lng + lnb
        s_ref[r, :] = (yn * jax.nn.sigmoid(yn)).astype(BF16)
        return carry

    lax.fori_loop(0, ts // rows, ln_body, 0)
    o_ref[...] = (x_ref[...] + jnp.dot(s_ref[...], wout_ref[...], preferred_element_type=F32)
                  + bout_ref[...])


def _conv_tail(g, x, w_dw, b_dw, ln_g, ln_b, w_out, b_out, seq, ts):
    m, d = g.shape
    kw = w_dw.shape[0]
    assert kw - 1 <= CONV_HALO and seq % ts == 0 and ts % CONV_HALO == 0
    halo_per_tile = ts // CONV_HALO
    row = lambda v: v.reshape(1, d)
    const = lambda i: (0, 0)
    return pl.pallas_call(
        functools.partial(_conv_body, seq_tiles=seq // ts, row_chunk=64, lane_chunk=256),
        name="conv_tail",
        grid=(m // ts,),
        in_specs=[pl.BlockSpec((ts, d), lambda i: (i, 0)),
                  pl.BlockSpec((CONV_HALO, d), lambda i: (jnp.maximum(i * halo_per_tile - 1, 0), 0)),
                  pl.BlockSpec((ts, d), lambda i: (i, 0)),
                  pl.BlockSpec((kw, d), const),
                  pl.BlockSpec((1, d), const), pl.BlockSpec((1, d), const), pl.BlockSpec((1, d), const),
                  pl.BlockSpec((d, d), const),
                  pl.BlockSpec((1, d), const)],
        out_specs=pl.BlockSpec((ts, d), lambda i: (i, 0)),
        out_shape=jax.ShapeDtypeStruct((m, d), F32),
        scratch_shapes=[pltpu.VMEM((ts + CONV_HALO, d), F32),
                        pltpu.VMEM((ts, d), F32),
                        pltpu.VMEM((ts, d), BF16)],
        compiler_params=_params("arbitrary"),
    )(g, g, x, w_dw, row(b_dw), row(ln_g), row(ln_b), w_out, row(b_out))


def _mlstm_body(q_ref, k_ref, v_ref, o_ref, gate_ref, wq_ref, wk_ref, bq_ref, bk_ref, ng_ref,
                y_ref, qe_ref, ke_ref, c_ref, n_ref, m_ref, *, qk_scale):
    L = q_ref.shape[0]
    kw = wq_ref.shape[0]

    @pl.when(pl.program_id(2) == 0)
    def _():
        qe_ref[0:QK_HALO, :] = jnp.zeros((QK_HALO, qe_ref.shape[1]), F32)
        ke_ref[0:QK_HALO, :] = jnp.zeros((QK_HALO, ke_ref.shape[1]), F32)
        c_ref[...] = jnp.zeros(c_ref.shape, F32)
        n_ref[...] = jnp.zeros(n_ref.shape, F32)
        m_ref[...] = jnp.zeros(m_ref.shape, F32)

    def short_conv(src_ref, e_ref, w_ref, b_ref):
        e_ref[QK_HALO:, :] = src_ref[...]
        acc = b_ref[...]
        for j in range(kw):
            start = QK_HALO - (kw - 1) + j
            acc = acc + w_ref[j:j + 1, :] * e_ref[start:start + L, :]
        e_ref[0:QK_HALO, :] = e_ref[L:L + QK_HALO, :]
        return acc * jax.nn.sigmoid(acc)

    q = short_conv(q_ref, qe_ref, wq_ref, bq_ref) * qk_scale
    k = short_conv(k_ref, ke_ref, wk_ref, bk_ref)
    v = v_ref[...]

    ig_r = gate_ref[0:1, :]
    lf_r = jax.nn.log_sigmoid(gate_ref[1:2, :])
    t_idx = lax.broadcasted_iota(jnp.int32, (L, L), 0)
    s_idx = lax.broadcasted_iota(jnp.int32, (L, L), 1)
    causal = s_idx <= t_idx
    diag = s_idx == t_idx
    lf_c = jnp.sum(jnp.where(diag, lf_r, 0.0), axis=1, keepdims=True)
    ig_c = jnp.sum(jnp.where(diag, ig_r, 0.0), axis=1, keepdims=True)
    b_c = jnp.sum(jnp.where(causal, lf_r, 0.0), axis=1, keepdims=True)
    b_r = jnp.sum(jnp.where(t_idx <= s_idx, lf_c, 0.0), axis=0, keepdims=True)
    a_r = ig_r - b_r
    a_c = ig_c - b_c
    a_mat = jnp.where(causal, a_r, -jnp.inf)
    m_prev = m_ref[0:1, 0:1]
    g_c = jnp.maximum(m_prev, jnp.max(a_mat, axis=1, keepdims=True))
    d_mat = jnp.exp(a_mat - g_c)
    inter_c = jnp.exp(m_prev - g_c)
    m_c = b_c + g_c
    b_last = b_c[L - 1:L, :]
    m_new = m_c[L - 1:L, :]

    qb = q.astype(BF16)
    kb = k.astype(BF16)
    s = lax.dot_general(qb, kb, (((1,), (1,)), ((), ())), preferred_element_type=F32) * d_mat
    num = (jnp.dot(s.astype(BF16), v.astype(BF16), preferred_element_type=F32)
           + inter_c * jnp.dot(qb, c_ref[...].astype(BF16), preferred_element_type=F32))
    den = (jnp.sum(s, axis=1, keepdims=True)
           + inter_c * jnp.sum(q * n_ref[...], axis=1, keepdims=True))
    h = num / jnp.maximum(jnp.abs(den), jnp.exp(-m_c))

    w_c = jnp.exp(b_last + a_c - m_new)
    decay = jnp.exp(b_last + m_prev - m_new)
    c_ref[...] = decay * c_ref[...] + lax.dot_general(
        kb, (w_c * v).astype(BF16), (((0,), (0,)), ((), ())), preferred_element_type=F32)
    n_ref[...] = decay * n_ref[...] + jnp.sum(w_c * k, axis=0, keepdims=True)
    m_ref[...] = jnp.broadcast_to(m_new, m_ref.shape)

    hn = h * lax.rsqrt(jnp.mean(h * h, axis=1, keepdims=True) + EPS) * ng_ref[...]
    y_ref[...] = (jax.nn.sigmoid(o_ref[...]) * hn).astype(y_ref.dtype)


def _mlstm_recurrence(p, gates, w_qkconv, b_qkconv, norm_g, batch, seq):
    heads = MLSTM_HEADS
    L = MLSTM_CHUNK
    hqk = w_qkconv.shape[1] // 2
    hv = norm_g.shape[0]
    dqk, dv = hqk // heads, hv // heads
    nc = seq // L
    p3 = p.reshape(batch, seq, p.shape[1])
    k_blk, v_blk, o_blk = hqk // dqk, 2 * hqk // dv, (2 * hqk + hv) // dv
    kw = w_qkconv.shape[0]
    bqk = b_qkconv.reshape(1, -1)
    return pl.pallas_call(
        functools.partial(_mlstm_body, qk_scale=dqk ** -0.5),
        name="mlstm_recurrence",
        grid=(batch, heads, nc),
        in_specs=[pl.BlockSpec((None, L, dqk), lambda b, h, c: (b, c, h)),
                  pl.BlockSpec((None, L, dqk), lambda b, h, c: (b, c, k_blk + h)),
                  pl.BlockSpec((None, L, dv), lambda b, h, c: (b, c, v_blk + h)),
                  pl.BlockSpec((None, L, dv), lambda b, h, c: (b, c, o_blk + h)),
                  pl.BlockSpec((None, None, None, 2, L), lambda b, h, c: (b, h, c, 0, 0)),
                  pl.BlockSpec((kw, dqk), lambda b, h, c: (0, h)),
                  pl.BlockSpec((kw, dqk), lambda b, h, c: (0, k_blk + h)),
                  pl.BlockSpec((1, dqk), lambda b, h, c: (0, h)),
                  pl.BlockSpec((1, dqk), lambda b, h, c: (0, k_blk + h)),
                  pl.BlockSpec((1, dv), lambda b, h, c: (0, h))],
        out_specs=pl.BlockSpec((None, L, dv), lambda b, h, c: (b, c, h)),
        out_shape=jax.ShapeDtypeStruct((batch, seq, hv), BF16),
        scratch_shapes=[pltpu.VMEM((L + QK_HALO, dqk), F32),
                        pltpu.VMEM((L + QK_HALO, dqk), F32),
                        pltpu.VMEM((dqk, dv), F32),
                        pltpu.VMEM((1, dqk), F32),
                        pltpu.VMEM((8, 128), F32)],
        compiler_params=_params("parallel", "parallel", "arbitrary"),
    )(p3, p3, p3, p3, gates, w_qkconv, w_qkconv, bqk, bqk, norm_g.reshape(1, hv)).reshape(batch * seq, hv)


def _sgu_body(u_ref, v_ref, x_ref, ng_ref, ws_ref, bst_ref, wout_ref, bout_ref, o_ref,
              vn_ref, gated_ref):
    tm, width = u_ref.shape
    groups, chunk, _ = ws_ref.shape
    gw = width // groups
    _rmsnorm_rows(v_ref, ng_ref, vn_ref)
    t_idx = lax.broadcasted_iota(jnp.int32, (chunk, chunk), 0)
    s_idx = lax.broadcasted_iota(jnp.int32, (chunk, chunk), 1)
    tril = (s_idx <= t_idx).astype(F32)
    for g in range(groups):
        ws_g = (ws_ref[g] * tril).astype(BF16)
        bias_c = bst_ref[:, g:g + 1]
        cols = slice(g * gw, (g + 1) * gw)
        for c in range(tm // chunk):
            rows = slice(c * chunk, (c + 1) * chunk)
            mixed = jnp.dot(ws_g, vn_ref[rows, cols], preferred_element_type=F32) + bias_c
            gated_ref[rows, cols] = (u_ref[rows, cols] * mixed).astype(BF16)
    o_ref[...] = (x_ref[...] + jnp.dot(gated_ref[...], wout_ref[...], preferred_element_type=F32)
                  + bout_ref[...])


def _sgu_tail(p, x, norm_g, w_s, b_s, w_out, b_out, tm):
    m, d = x.shape
    width = norm_g.shape[0]
    groups, chunk, _ = w_s.shape
    const2 = lambda i: (0, 0)
    return pl.pallas_call(
        _sgu_body,
        name="sgu_tail",
        grid=(m // tm,),
        in_specs=[pl.BlockSpec((tm, width), lambda i: (i, 0)),
                  pl.BlockSpec((tm, width), lambda i: (i, 1)),
                  pl.BlockSpec((tm, d), lambda i: (i, 0)),
                  pl.BlockSpec((1, width), const2),
                  pl.BlockSpec((groups, chunk, chunk), lambda i: (0, 0, 0)),
                  pl.BlockSpec((chunk, groups), const2),
                  pl.BlockSpec((width, d), const2),
                  pl.BlockSpec((1, d), const2)],
        out_specs=pl.BlockSpec((tm, d), lambda i: (i, 0)),
        out_shape=jax.ShapeDtypeStruct((m, d), F32),
        scratch_shapes=[pltpu.VMEM((tm, width), BF16), pltpu.VMEM((tm, width), BF16)],
        compiler_params=_params("parallel"),
    )(p, p, x, norm_g.reshape(1, width), w_s, jnp.transpose(b_s), w_out, b_out.reshape(1, d))


def _glu(val, gate):
    return val * jax.nn.sigmoid(gate)


def _identity(a):
    return a


def _gelu(a):
    return 0.5 * a * (1.0 + lax.erf(a * (2.0 ** -0.5)))


def _conformer_layer(x, ng, w_in, b_in, w_dw, b_dw, ln_g, ln_b, w_out, b_out, seq):
    d = x.shape[1]
    g = _norm_matmul(x, ng, w_in.astype(BF16), b_in, (0, d), d, _glu, tm=1024, tn=512)
    return _conv_tail(g, x, w_dw, b_dw, ln_g, ln_b, w_out.astype(BF16), b_out, seq, ts=512)


def _mlstm_layer(x, ng, w_in, b_in, w_qkconv, b_qkconv, norm_g, w_out, batch, seq):
    heads = MLSTM_HEADS
    n_main = w_in.shape[1] - 2 * heads
    p = _norm_matmul(x, ng, w_in[:, :n_main].astype(BF16), b_in[:n_main], (0,), n_main, _identity,
                     tm=1024, tn=512)
    lane = 128
    w_gate = jnp.pad(w_in[:, n_main:], ((0, 0), (0, lane - 2 * heads))).astype(BF16)
    b_gate = jnp.pad(b_in[n_main:], (0, lane - 2 * heads))
    gates = _norm_matmul(x, ng, w_gate, b_gate, (0,), lane, _identity, tm=1024, tn=lane)
    nc = seq // MLSTM_CHUNK
    gates = gates[:, :2 * heads].reshape(batch, nc, MLSTM_CHUNK, 2, heads)
    gates = jnp.transpose(gates, (0, 4, 1, 3, 2))
    y = _mlstm_recurrence(p, gates, w_qkconv, b_qkconv, norm_g, batch, seq)
    return _matmul_residual(y, w_out.astype(BF16), x, tm=1024, tn=512)


def _sgu_layer(x, ng, w_in, b_in, norm_g, w_s, b_s, w_out, b_out):
    p = _norm_matmul(x, ng, w_in.astype(BF16), b_in, (0,), w_in.shape[1], _gelu, tm=1024, tn=512)
    return _sgu_tail(p, x, norm_g, w_s, b_s, w_out.astype(BF16), b_out, tm=512)


def kernel(x, norm_mix_g, norm_ffn_g, final_g, conv_w_in, conv_b_in, conv_w_dw, conv_b_dw, conv_ln_g, conv_ln_b, conv_w_out, conv_b_out, mlstm_w_in, mlstm_b_in, mlstm_w_qkconv, mlstm_b_qkconv, mlstm_norm_g, mlstm_w_out, sgu_w_in, sgu_b_in, sgu_norm_g, sgu_w_s, sgu_b_s, sgu_w_out, sgu_b_out, ffn_w1, ffn_w2):
    batch, seq, d = x.shape
    depth = norm_mix_g.shape[0]
    h = x.reshape(batch * seq, d)
    for i in range(depth):
        kind, j = i % N_MIXERS, i // N_MIXERS
        if kind == 0:
            h = _conformer_layer(h, norm_mix_g[i], conv_w_in[j], conv_b_in[j], conv_w_dw[j], conv_b_dw[j],
                                 conv_ln_g[j], conv_ln_b[j], conv_w_out[j], conv_b_out[j], seq)
        elif kind == 1:
            h = _mlstm_layer(h, norm_mix_g[i], mlstm_w_in[j], mlstm_b_in[j], mlstm_w_qkconv[j],
                             mlstm_b_qkconv[j], mlstm_norm_g[j], mlstm_w_out[j], batch, seq)
        else:
            h = _sgu_layer(h, norm_mix_g[i], sgu_w_in[j], sgu_b_in[j], sgu_norm_g[j], sgu_w_s[j], sgu_b_s[j],
                           sgu_w_out[j], sgu_b_out[j])
        h = _ffn(h, norm_ffn_g[i], ffn_w1[i].astype(BF16), ffn_w2[i].astype(BF16), final_g,
                 final_norm=(i == depth - 1), tm=512, tf=512)
    return h.reshape(batch, seq, d)
```

```python
import functools

import jax
import jax.numpy as jnp
from jax import lax
from jax.experimental import pallas as pl
from jax.experimental.pallas import tpu as pltpu

EPS = 1e-6
N_MIXERS = 3
MLSTM_HEADS = 8
MLSTM_CHUNK = 256
QK_HALO = 8
SGU_CHUNK = 128
SGU_GROUPS = 8
CONV_HALO = 32
LANES = 128
FFN_TILES = ((512, 512), (1024, 512), (512, 1024), (1024, 256))
GLU_TILES = ((1024, 512), (512, 1024))

F32 = jnp.float32
BF16 = jnp.bfloat16

V7X_VMEM_BYTES = 64 * 1024 * 1024
VMEM_LIMIT = V7X_VMEM_BYTES - 8 * 1024 * 1024


def _params(*sem):
    return pltpu.CompilerParams(dimension_semantics=sem, vmem_limit_bytes=VMEM_LIMIT)


def _rmsnorm_rows(x_ref, g_ref, h_ref, rows=256):
    g = g_ref[...]

    def body(i, carry):
        r = pl.ds(pl.multiple_of(i * rows, rows), rows)
        x = x_ref[r, :]
        ms = jnp.mean(x * x, axis=-1, keepdims=True)
        h_ref[r, :] = (x * lax.rsqrt(ms + EPS) * g).astype(h_ref.dtype)
        return carry

    lax.fori_loop(0, x_ref.shape[0] // rows, body, 0)


def _norm_mm_body(*refs, n_w, epilogue):
    x_ref, g_ref = refs[0], refs[1]
    w_refs = refs[2:2 + n_w]
    b_refs = refs[2 + n_w:2 + 2 * n_w]
    o_ref, h_ref = refs[2 + 2 * n_w], refs[3 + 2 * n_w]

    @pl.when(pl.program_id(1) == 0)
    def _():
        _rmsnorm_rows(x_ref, g_ref, h_ref)

    h = h_ref[...]
    accs = [jnp.dot(h, w[...], preferred_element_type=F32) + b[...]
            for w, b in zip(w_refs, b_refs)]
    o_ref[...] = epilogue(*accs).astype(o_ref.dtype)


def _norm_matmul(x, g, w, b, col_offsets, n_out, epilogue, tm, tn, out_dtype=F32):
    m, k = x.shape
    n_w = len(col_offsets)
    w_specs = [pl.BlockSpec((k, tn), functools.partial(lambda i, j, o: (0, j + o), o=off // tn))
               for off in col_offsets]
    b_specs = [pl.BlockSpec((1, tn), functools.partial(lambda i, j, o: (0, j + o), o=off // tn))
               for off in col_offsets]
    return pl.pallas_call(
        functools.partial(_norm_mm_body, n_w=n_w, epilogue=epilogue),
        name="norm_matmul_" + epilogue.__name__.strip("_"),
        grid=(m // tm, n_out // tn),
        in_specs=[pl.BlockSpec((tm, k), lambda i, j: (i, 0)),
                  pl.BlockSpec((1, k), lambda i, j: (0, 0))] + w_specs + b_specs,
        out_specs=pl.BlockSpec((tm, tn), lambda i, j: (i, j)),
        out_shape=jax.ShapeDtypeStruct((m, n_out), out_dtype),
        scratch_shapes=[pltpu.VMEM((tm, k), BF16)],
        compiler_params=_params("parallel", "arbitrary"),
    )(x, g.reshape(1, k), *([w] * n_w), *([b.reshape(1, -1)] * n_w))


def _mm_res_body(a_ref, w_ref, x_ref, o_ref):
    o_ref[...] = x_ref[...] + jnp.dot(a_ref[...], w_ref[...], preferred_element_type=F32)


def _matmul_residual(a, w, x, tm, tn):
    m, k = a.shape
    n = w.shape[1]
    return pl.pallas_call(
        _mm_res_body,
        name="matmul_residual",
        grid=(m // tm, n // tn),
        in_specs=[pl.BlockSpec((tm, k), lambda i, j: (i, 0)),
                  pl.BlockSpec((k, tn), lambda i, j: (0, j)),
                  pl.BlockSpec((tm, tn), lambda i, j: (i, j))],
        out_specs=pl.BlockSpec((tm, tn), lambda i, j: (i, j)),
        out_shape=jax.ShapeDtypeStruct((m, n), F32),
        compiler_params=_params("parallel", "arbitrary"),
    )(a, w, x)


def _ffn_body(x_ref, g_ref, w1_ref, w2_ref, fg_ref, o_ref, h_ref, *, final_norm):
    j = pl.program_id(1)

    @pl.when(j == 0)
    def _():
        _rmsnorm_rows(x_ref, g_ref, h_ref)
        o_ref[...] = x_ref[...]

    a = jnp.dot(h_ref[...], w1_ref[...], preferred_element_type=F32)
    a = jnp.square(jnp.maximum(a, 0.0)).astype(BF16)
    o_ref[...] += jnp.dot(a, w2_ref[...], preferred_element_type=F32)

    if final_norm:
        @pl.when(j == pl.num_programs(1) - 1)
        def _():
            _rmsnorm_rows(o_ref, fg_ref, o_ref)


def _ffn(x, g, w1, w2, final_g, final_norm, tm, tf):
    m, d = x.shape
    f = w1.shape[1]
    return pl.pallas_call(
        functools.partial(_ffn_body, final_norm=final_norm),
        name="ffn_final" if final_norm else "ffn",
        grid=(m // tm, f // tf),
        in_specs=[pl.BlockSpec((tm, d), lambda i, j: (i, 0)),
                  pl.BlockSpec((1, d), lambda i, j: (0, 0)),
                  pl.BlockSpec((d, tf), lambda i, j: (0, j)),
                  pl.BlockSpec((tf, d), lambda i, j: (j, 0)),
                  pl.BlockSpec((1, d), lambda i, j: (0, 0))],
        out_specs=pl.BlockSpec((tm, d), lambda i, j: (i, 0)),
        out_shape=jax.ShapeDtypeStruct((m, d), F32),
        scratch_shapes=[pltpu.VMEM((tm, d), BF16)],
        compiler_params=_params("parallel", "arbitrary"),
    )(x, g.reshape(1, d), w1, w2, final_g.reshape(1, d))


def _conv_body(gc_ref, gp_ref, x_ref, wdw_ref, bdw_ref, lng_ref, lnb_ref, wout_ref, bout_ref,
               o_ref, ext_ref, y_ref, s_ref, *, seq_tiles, row_chunk, lane_chunk):
    ts, d = gc_ref.shape
    kw = wdw_ref.shape[0]
    first = (pl.program_id(0) % seq_tiles) == 0
    lead = CONV_HALO - (kw - 1)

    for p in range(d // lane_chunk):
        lanes = slice(p * lane_chunk, (p + 1) * lane_chunk)
        ext_ref[p, 0:CONV_HALO, :] = jnp.where(first, 0.0, gp_ref[:, lanes])
        ext_ref[p, CONV_HALO:, :] = gc_ref[:, lanes]
        w_rows = [wdw_ref[k:k + 1, lanes] for k in range(kw)]
        bias = jnp.broadcast_to(bdw_ref[:, lanes], (row_chunk, lane_chunk))

        def row_body(rc, carry, p=p, lanes=lanes, w_rows=w_rows, bias=bias):
            base = pl.multiple_of(rc * row_chunk, row_chunk)
            acc = bias
            for k in range(kw):
                acc = acc + w_rows[k] * ext_ref[p, pl.ds(base + lead + k, row_chunk), :]
            y_ref[pl.ds(base, row_chunk), lanes] = acc
            return carry

        lax.fori_loop(0, ts // row_chunk, row_body, 0)

    lng = lng_ref[...]
    lnb = lnb_ref[...]
    rows = 256

    def ln_body(i, carry):
        r = pl.ds(pl.multiple_of(i * rows, rows), rows)
        y = y_ref[r, :]
        mu = jnp.mean(y, axis=-1, keepdims=True)
        yc = y - mu
        var = jnp.mean(yc * yc, axis=-1, keepdims=True)
        yn = yc * lax.rsqrt(var + EPS) * lng + lnb
        s_ref[r, :] = (yn * jax.nn.sigmoid(yn)).astype(BF16)
        return carry

    lax.fori_loop(0, ts // rows, ln_body, 0)
    o_ref[...] = (x_ref[...] + jnp.dot(s_ref[...], wout_ref[...], preferred_element_type=F32)
                  + bout_ref[...])


def _conv_tail(g, x, w_dw, b_dw, ln_g, ln_b, w_out, b_out, seq, ts):
    m, d = g.shape
    kw = w_dw.shape[0]
    assert kw - 1 <= CONV_HALO and seq % ts == 0 and ts % CONV_HALO == 0
    halo_per_tile = ts // CONV_HALO
    row = lambda v: v.reshape(1, d)
    const = lambda i: (0, 0)
    return pl.pallas_call(
        functools.partial(_conv_body, seq_tiles=seq // ts, row_chunk=64, lane_chunk=LANES),
        name="conv_tail",
        grid=(m // ts,),
        in_specs=[pl.BlockSpec((ts, d), lambda i: (i, 0)),
                  pl.BlockSpec((CONV_HALO, d), lambda i: (jnp.maximum(i * halo_per_tile - 1, 0), 0)),
                  pl.BlockSpec((ts, d), lambda i: (i, 0)),
                  pl.BlockSpec((kw, d), const),
                  pl.BlockSpec((1, d), const), pl.BlockSpec((1, d), const), pl.BlockSpec((1, d), const),
                  pl.BlockSpec((d, d), const),
                  pl.BlockSpec((1, d), const)],
        out_specs=pl.BlockSpec((ts, d), lambda i: (i, 0)),
        out_shape=jax.ShapeDtypeStruct((m, d), F32),
        scratch_shapes=[pltpu.VMEM((d // LANES, ts + CONV_HALO, LANES), F32),
                        pltpu.VMEM((ts, d), F32),
                        pltpu.VMEM((ts, d), BF16)],
        compiler_params=_params("arbitrary"),
    )(g, g, x, w_dw, row(b_dw), row(ln_g), row(ln_b), w_out, row(b_out))


def _mlstm_body(q_ref, k_ref, v_ref, o_ref, gate_ref, wq_ref, wk_ref, bq_ref, bk_ref, ng_ref,
                y_ref, qe_ref, ke_ref, c_ref, n_ref, m_ref, *, qk_scale):
    L = q_ref.shape[0]
    kw = wq_ref.shape[0]

    @pl.when(pl.program_id(2) == 0)
    def _():
        qe_ref[:, 0:QK_HALO, :] = jnp.zeros((qe_ref.shape[0], QK_HALO, LANES), F32)
        ke_ref[:, 0:QK_HALO, :] = jnp.zeros((ke_ref.shape[0], QK_HALO, LANES), F32)
        c_ref[...] = jnp.zeros(c_ref.shape, F32)
        n_ref[...] = jnp.zeros(n_ref.shape, F32)
        m_ref[...] = jnp.zeros(m_ref.shape, F32)

    def short_conv(src_ref, e_ref, w_ref, b_ref):
        outs = []
        for p in range(e_ref.shape[0]):
            lanes = slice(p * LANES, (p + 1) * LANES)
            e_ref[p, QK_HALO:, :] = src_ref[:, lanes]
            acc = b_ref[:, lanes]
            for j in range(kw):
                start = QK_HALO - (kw - 1) + j
                acc = acc + w_ref[j:j + 1, lanes] * e_ref[p, start:start + L, :]
            e_ref[p, 0:QK_HALO, :] = e_ref[p, L:L + QK_HALO, :]
            outs.append(acc * jax.nn.sigmoid(acc))
        return jnp.concatenate(outs, axis=1)

    q = short_conv(q_ref, qe_ref, wq_ref, bq_ref) * qk_scale
    k = short_conv(k_ref, ke_ref, wk_ref, bk_ref)
    v = v_ref[...]

    ig_r = gate_ref[0:1, :]
    lf_r = jax.nn.log_sigmoid(gate_ref[1:2, :])
    t_idx = lax.broadcasted_iota(jnp.int32, (L, L), 0)
    s_idx = lax.broadcasted_iota(jnp.int32, (L, L), 1)
    causal = s_idx <= t_idx
    diag = s_idx == t_idx
    lf_c = jnp.sum(jnp.where(diag, lf_r, 0.0), axis=1, keepdims=True)
    ig_c = jnp.sum(jnp.where(diag, ig_r, 0.0), axis=1, keepdims=True)
    b_c = jnp.sum(jnp.where(causal, lf_r, 0.0), axis=1, keepdims=True)
    b_r = jnp.sum(jnp.where(t_idx <= s_idx, lf_c, 0.0), axis=0, keepdims=True)
    a_r = ig_r - b_r
    a_c = ig_c - b_c
    a_mat = jnp.where(causal, a_r, -jnp.inf)
    m_prev = m_ref[0:1, 0:1]
    g_c = jnp.maximum(m_prev, jnp.max(a_mat, axis=1, keepdims=True))
    d_mat = jnp.exp(a_mat - g_c)
    inter_c = jnp.exp(m_prev - g_c)
    m_c = b_c + g_c
    b_last = b_c[L - 1:L, :]
    m_new = m_c[L - 1:L, :]

    qb = q.astype(BF16)
    kb = k.astype(BF16)
    s = lax.dot_general(qb, kb, (((1,), (1,)), ((), ())), preferred_element_type=F32) * d_mat
    num = (jnp.dot(s.astype(BF16), v.astype(BF16), preferred_element_type=F32)
           + inter_c * jnp.dot(qb, c_ref[...].astype(BF16), preferred_element_type=F32))
    den = (jnp.sum(s, axis=1, keepdims=True)
           + inter_c * jnp.sum(q * n_ref[...], axis=1, keepdims=True))
    h = num / jnp.maximum(jnp.abs(den), jnp.exp(-m_c))

    w_c = jnp.exp(b_last + a_c - m_new)
    decay = jnp.exp(b_last + m_prev - m_new)
    c_ref[...] = decay * c_ref[...] + lax.dot_general(
        kb, (w_c * v).astype(BF16), (((0,), (0,)), ((), ())), preferred_element_type=F32)
    n_ref[...] = decay * n_ref[...] + jnp.sum(w_c * k, axis=0, keepdims=True)
    m_ref[...] = jnp.broadcast_to(m_new, m_ref.shape)

    hn = h * lax.rsqrt(jnp.mean(h * h, axis=1, keepdims=True) + EPS) * ng_ref[...]
    y_ref[...] = (jax.nn.sigmoid(o_ref[...]) * hn).astype(y_ref.dtype)


def _mlstm_recurrence(p, gates, w_qkconv, b_qkconv, norm_g, batch, seq):
    heads = MLSTM_HEADS
    L = MLSTM_CHUNK
    hqk = w_qkconv.shape[1] // 2
    hv = norm_g.shape[0]
    dqk, dv = hqk // heads, hv // heads
    nc = seq // L
    p3 = p.reshape(batch, seq, p.shape[1])
    k_blk, v_blk, o_blk = hqk // dqk, 2 * hqk // dv, (2 * hqk + hv) // dv
    kw = w_qkconv.shape[0]
    bqk = b_qkconv.reshape(1, -1)
    return pl.pallas_call(
        functools.partial(_mlstm_body, qk_scale=dqk ** -0.5),
        name="mlstm_recurrence",
        grid=(batch, heads, nc),
        in_specs=[pl.BlockSpec((None, L, dqk), lambda b, h, c: (b, c, h)),
                  pl.BlockSpec((None, L, dqk), lambda b, h, c: (b, c, k_blk + h)),
                  pl.BlockSpec((None, L, dv), lambda b, h, c: (b, c, v_blk + h)),
                  pl.BlockSpec((None, L, dv), lambda b, h, c: (b, c, o_blk + h)),
                  pl.BlockSpec((None, None, None, 2, L), lambda b, h, c: (b, h, c, 0, 0)),
                  pl.BlockSpec((kw, dqk), lambda b, h, c: (0, h)),
                  pl.BlockSpec((kw, dqk), lambda b, h, c: (0, k_blk + h)),
                  pl.BlockSpec((1, dqk), lambda b, h, c: (0, h)),
                  pl.BlockSpec((1, dqk), lambda b, h, c: (0, k_blk + h)),
                  pl.BlockSpec((1, dv), lambda b, h, c: (0, h))],
        out_specs=pl.BlockSpec((None, L, dv), lambda b, h, c: (b, c, h)),
        out_shape=jax.ShapeDtypeStruct((batch, seq, hv), BF16),
        scratch_shapes=[pltpu.VMEM((dqk // LANES, L + QK_HALO, LANES), F32),
                        pltpu.VMEM((dqk // LANES, L + QK_HALO, LANES), F32),
                        pltpu.VMEM((dqk, dv), F32),
                        pltpu.VMEM((1, dqk), F32),
                        pltpu.VMEM((8, 128), F32)],
        compiler_params=_params("parallel", "parallel", "arbitrary"),
    )(p3, p3, p3, p3, gates, w_qkconv, w_qkconv, bqk, bqk, norm_g.reshape(1, hv)).reshape(batch * seq, hv)


def _sgu_body(u_ref, v_ref, x_ref, ng_ref, ws_ref, bst_ref, wout_ref, bout_ref, o_ref,
              vn_ref, gated_ref):
    tm, width = u_ref.shape
    groups, chunk, _ = ws_ref.shape
    gw = width // groups
    _rmsnorm_rows(v_ref, ng_ref, vn_ref)
    t_idx = lax.broadcasted_iota(jnp.int32, (chunk, chunk), 0)
    s_idx = lax.broadcasted_iota(jnp.int32, (chunk, chunk), 1)
    tril = (s_idx <= t_idx).astype(F32)
    for g in range(groups):
        ws_g = (ws_ref[g] * tril).astype(BF16)
        bias_c = bst_ref[:, g:g + 1]
        cols = slice(g * gw, (g + 1) * gw)
        for c in range(tm // chunk):
            rows = slice(c * chunk, (c + 1) * chunk)
            mixed = jnp.dot(ws_g, vn_ref[rows, cols], preferred_element_type=F32) + bias_c
            gated_ref[rows, cols] = (u_ref[rows, cols] * mixed).astype(BF16)
    o_ref[...] = (x_ref[...] + jnp.dot(gated_ref[...], wout_ref[...], preferred_element_type=F32)
                  + bout_ref[...])


def _sgu_tail(p, x, norm_g, w_s, b_s, w_out, b_out, tm):
    m, d = x.shape
    width = norm_g.shape[0]
    groups, chunk, _ = w_s.shape
    const2 = lambda i: (0, 0)
    return pl.pallas_call(
        _sgu_body,
        name="sgu_tail",
        grid=(m // tm,),
        in_specs=[pl.BlockSpec((tm, width), lambda i: (i, 0)),
                  pl.BlockSpec((tm, width), lambda i: (i, 1)),
                  pl.BlockSpec((tm, d), lambda i: (i, 0)),
                  pl.BlockSpec((1, width), const2),
                  pl.BlockSpec((groups, chunk, chunk), lambda i: (0, 0, 0)),
                  pl.BlockSpec((chunk, groups), const2),
                  pl.BlockSpec((width, d), const2),
                  pl.BlockSpec((1, d), const2)],
        out_specs=pl.BlockSpec((tm, d), lambda i: (i, 0)),
        out_shape=jax.ShapeDtypeStruct((m, d), F32),
        scratch_shapes=[pltpu.VMEM((tm, width), BF16), pltpu.VMEM((tm, width), BF16)],
        compiler_params=_params("parallel"),
    )(p, p, x, norm_g.reshape(1, width), w_s, jnp.transpose(b_s), w_out, b_out.reshape(1, d))


def _glu(val, gate):
    return val * jax.nn.sigmoid(gate)


def _identity(a):
    return a


def _gelu(a):
    return 0.5 * a * (1.0 + lax.erf(a * (2.0 ** -0.5)))


def _conformer_layer(x, ng, w_in, b_in, w_dw, b_dw, ln_g, ln_b, w_out, b_out, seq, tiles):
    d = x.shape[1]
    g = _norm_matmul(x, ng, w_in.astype(BF16), b_in, (0, d), d, _glu, tm=tiles[0], tn=tiles[1])
    return _conv_tail(g, x, w_dw, b_dw, ln_g, ln_b, w_out.astype(BF16), b_out, seq, ts=512)


def _mlstm_layer(x, ng, w_in, b_in, w_qkconv, b_qkconv, norm_g, w_out, batch, seq):
    heads = MLSTM_HEADS
    n_main = w_in.shape[1] - 2 * heads
    p = _norm_matmul(x, ng, w_in.astype(BF16), b_in, (0,), n_main, _identity, tm=1024, tn=512)
    lane = 128
    w_gate = jnp.pad(w_in[:, n_main:], ((0, 0), (0, lane - 2 * heads))).astype(BF16)
    b_gate = jnp.pad(b_in[n_main:], (0, lane - 2 * heads))
    gates = _norm_matmul(x, ng, w_gate, b_gate, (0,), lane, _identity, tm=1024, tn=lane)
    nc = seq // MLSTM_CHUNK
    gates = gates[:, :2 * heads].reshape(batch, nc, MLSTM_CHUNK, 2, heads)
    gates = jnp.transpose(gates, (0, 4, 1, 3, 2))
    y = _mlstm_recurrence(p, gates, w_qkconv, b_qkconv, norm_g, batch, seq)
    return _matmul_residual(y, w_out.astype(BF16), x, tm=1024, tn=512)


def _sgu_layer(x, ng, w_in, b_in, norm_g, w_s, b_s, w_out, b_out):
    p = _norm_matmul(x, ng, w_in.astype(BF16), b_in, (0,), w_in.shape[1], _gelu, tm=1024, tn=512)
    return _sgu_tail(p, x, norm_g, w_s, b_s, w_out.astype(BF16), b_out, tm=512)


def kernel(x, norm_mix_g, norm_ffn_g, final_g, conv_w_in, conv_b_in, conv_w_dw, conv_b_dw, conv_ln_g, conv_ln_b, conv_w_out, conv_b_out, mlstm_w_in, mlstm_b_in, mlstm_w_qkconv, mlstm_b_qkconv, mlstm_norm_g, mlstm_w_out, sgu_w_in, sgu_b_in, sgu_norm_g, sgu_w_s, sgu_b_s, sgu_w_out, sgu_b_out, ffn_w1, ffn_w2):
    batch, seq, d = x.shape
    depth = norm_mix_g.shape[0]
    h = x.reshape(batch * seq, d)
    for i in range(depth):
        kind, j = i % N_MIXERS, i // N_MIXERS
        if kind == 0:
            h = _conformer_layer(h, norm_mix_g[i], conv_w_in[j], conv_b_in[j], conv_w_dw[j], conv_b_dw[j],
                                 conv_ln_g[j], conv_ln_b[j], conv_w_out[j], conv_b_out[j], seq,
                                 tiles=GLU_TILES[j % len(GLU_TILES)])
        elif kind == 1:
            h = _mlstm_layer(h, norm_mix_g[i], mlstm_w_in[j], mlstm_b_in[j], mlstm_w_qkconv[j],
                             mlstm_b_qkconv[j], mlstm_norm_g[j], mlstm_w_out[j], batch, seq)
        else:
            h = _sgu_layer(h, norm_mix_g[i], sgu_w_in[j], sgu_b_in[j], sgu_norm_g[j], sgu_w_s[j], sgu_b_s[j],
                           sgu_w_out[j], sgu_b_out[j])
        h = _ffn(h, norm_ffn_g[i], ffn_w1[i].astype(BF16), ffn_w2[i].astype(BF16), final_g,
                 final_norm=(i == depth - 1), tm=FFN_TILES[i % len(FFN_TILES)][0],
                 tf=FFN_TILES[i % len(FFN_TILES)][1])
    return h.reshape(batch, seq, d)
```

```python
import functools

import jax
import jax.numpy as jnp
from jax import lax
from jax.experimental import pallas as pl
from jax.experimental.pallas import tpu as pltpu

EPS = 1e-6
N_MIXERS = 3
MLSTM_HEADS = 8
MLSTM_CHUNK = 256
SGU_CHUNK = 128
SGU_GROUPS = 8
CONV_HALO = 32
QK_HALO = 16
LANES = 128
SUB_ROWS = 256

F32 = jnp.float32
BF16 = jnp.bfloat16

V7X_VMEM_BYTES = 64 * 1024 * 1024
VMEM_LIMIT = V7X_VMEM_BYTES - 8 * 1024 * 1024

TILES = dict(
    ffn=(1024, 1024),
    proj=(1024, 512),
    out=(1024, 512),
    conv_rows=512,
    sgu_rows=512,
)


def _params(*sem):
    return pltpu.CompilerParams(dimension_semantics=sem, vmem_limit_bytes=VMEM_LIMIT)


def _rmsnorm_rows(x_ref, g_ref, h_ref, dst_offset=0, rows=256):
    g = g_ref[...]
    rows = min(rows, x_ref.shape[0])

    def body(i, carry):
        start = pl.multiple_of(i * rows, rows)
        x = x_ref[pl.ds(start, rows), :]
        ms = jnp.mean(x * x, axis=-1, keepdims=True)
        h_ref[pl.ds(start + dst_offset, rows), :] = (x * lax.rsqrt(ms + EPS) * g).astype(h_ref.dtype)
        return carry

    lax.fori_loop(0, x_ref.shape[0] // rows, body, 0)


def _norm_mm_body(*refs, n_w, epilogue):
    x_ref, g_ref = refs[0], refs[1]
    w_refs = refs[2:2 + n_w]
    b_refs = refs[2 + n_w:2 + 2 * n_w]
    o_ref, h_ref = refs[2 + 2 * n_w], refs[3 + 2 * n_w]

    @pl.when(pl.program_id(1) == 0)
    def _():
        _rmsnorm_rows(x_ref, g_ref, h_ref)

    for r in range(o_ref.shape[0] // SUB_ROWS):
        rows = slice(r * SUB_ROWS, (r + 1) * SUB_ROWS)
        h = h_ref[rows, :]
        accs = [jnp.dot(h, w[...], preferred_element_type=F32) + b[...]
                for w, b in zip(w_refs, b_refs)]
        o_ref[rows, :] = epilogue(*accs).astype(o_ref.dtype)


def _norm_matmul(x, g, w, b, layer, col_offsets, n_out, epilogue, out_dtype=F32, tn=None):
    m, k = x.shape
    tm, tn_default = TILES["proj"]
    tn = tn or tn_default
    n_w = len(col_offsets)
    col_map = lambda i, j, o: (layer, 0, j + o)
    w_specs = [pl.BlockSpec((None, k, tn), functools.partial(col_map, o=off // tn)) for off in col_offsets]
    b_specs = [pl.BlockSpec((None, 1, tn), functools.partial(col_map, o=off // tn)) for off in col_offsets]
    b3 = b.reshape(b.shape[0], 1, b.shape[1])
    return pl.pallas_call(
        functools.partial(_norm_mm_body, n_w=n_w, epilogue=epilogue),
        name="norm_matmul_" + epilogue.__name__.strip("_"),
        grid=(m // tm, n_out // tn),
        in_specs=[pl.BlockSpec((tm, k), lambda i, j: (i, 0)),
                  pl.BlockSpec((1, k), lambda i, j: (0, 0))] + w_specs + b_specs,
        out_specs=pl.BlockSpec((tm, tn), lambda i, j: (i, j)),
        out_shape=jax.ShapeDtypeStruct((m, n_out), out_dtype),
        scratch_shapes=[pltpu.VMEM((tm, k), BF16)],
        compiler_params=_params("parallel", "arbitrary"),
    )(x, g.reshape(1, k), *([w] * n_w), *([b3] * n_w))


def _qk_proj_body(x_ref, xh_ref, g_ref, w_ref, b_ref, cw_ref, cb_ref, o_ref, h_ref, p_ref,
                  *, seq_tiles, q_tiles, q_scale):
    i, j = pl.program_id(0), pl.program_id(1)
    tm, tn = o_ref.shape
    kw = cw_ref.shape[0]

    @pl.when(j == 0)
    def _():
        _rmsnorm_rows(xh_ref, g_ref, h_ref)
        _rmsnorm_rows(x_ref, g_ref, h_ref, dst_offset=QK_HALO)

    first = (i % seq_tiles) == 0
    scale = jnp.where(j < q_tiles, q_scale, 1.0)
    for r in range(tm // SUB_ROWS):
        lo = QK_HALO + r * SUB_ROWS
        src = 0 if r == 0 else lo
        acc = jnp.dot(h_ref[src:lo + SUB_ROWS, :], w_ref[...], preferred_element_type=F32) + b_ref[...]
        for p in range(tn // LANES):
            lanes = slice(p * LANES, (p + 1) * LANES)
            if r == 0:
                p_ref[p, 0:QK_HALO, :] = jnp.where(first, 0.0, acc[0:QK_HALO, lanes])
            p_ref[p, lo:lo + SUB_ROWS, :] = acc[lo - src:, lanes]
            y = cb_ref[:, lanes]
            for t in range(kw):
                start = lo - (kw - 1) + t
                y = y + cw_ref[t:t + 1, lanes] * p_ref[p, start:start + SUB_ROWS, :]
            o_ref[r * SUB_ROWS:(r + 1) * SUB_ROWS, lanes] = (y * jax.nn.sigmoid(y) * scale).astype(o_ref.dtype)


def _qk_projection(x, g, w, b, layer, conv_w, conv_b, n_q, q_scale, seq):
    m, k = x.shape
    tm, tn = TILES["proj"]
    n_out = conv_w.shape[1]
    kw = conv_w.shape[0]
    assert kw - 1 <= QK_HALO and seq % tm == 0 and tm % QK_HALO == 0
    halo_blocks = tm // QK_HALO
    b3 = b.reshape(b.shape[0], 1, b.shape[1])
    return pl.pallas_call(
        functools.partial(_qk_proj_body, seq_tiles=seq // tm, q_tiles=n_q // tn, q_scale=q_scale),
        name="mlstm_qk_projection",
        grid=(m // tm, n_out // tn),
        in_specs=[pl.BlockSpec((tm, k), lambda i, j: (i, 0)),
                  pl.BlockSpec((QK_HALO, k), lambda i, j: (jnp.maximum(i * halo_blocks - 1, 0), 0)),
                  pl.BlockSpec((1, k), lambda i, j: (0, 0)),
                  pl.BlockSpec((None, k, tn), lambda i, j: (layer, 0, j)),
                  pl.BlockSpec((None, 1, tn), lambda i, j: (layer, 0, j)),
                  pl.BlockSpec((kw, tn), lambda i, j: (0, j)),
                  pl.BlockSpec((1, tn), lambda i, j: (0, j))],
        out_specs=pl.BlockSpec((tm, tn), lambda i, j: (i, j)),
        out_shape=jax.ShapeDtypeStruct((m, n_out), BF16),
        scratch_shapes=[pltpu.VMEM((QK_HALO + tm, k), BF16),
                        pltpu.VMEM((tn // LANES, QK_HALO + tm, LANES), F32)],
        compiler_params=_params("parallel", "arbitrary"),
    )(x, x, g.reshape(1, k), w, b3, conv_w, conv_b.reshape(1, n_out))


def _mm_res_body(a_ref, w_ref, x_ref, o_ref):
    o_ref[...] = x_ref[...] + jnp.dot(a_ref[...], w_ref[...], preferred_element_type=F32)


def _matmul_residual(a, w, layer, x):
    m, k = a.shape
    n = w.shape[2]
    tm, tn = TILES["out"]
    return pl.pallas_call(
        _mm_res_body,
        name="matmul_residual",
        grid=(m // tm, n // tn),
        in_specs=[pl.BlockSpec((tm, k), lambda i, j: (i, 0)),
                  pl.BlockSpec((None, k, tn), lambda i, j: (layer, 0, j)),
                  pl.BlockSpec((tm, tn), lambda i, j: (i, j))],
        out_specs=pl.BlockSpec((tm, tn), lambda i, j: (i, j)),
        out_shape=jax.ShapeDtypeStruct((m, n), F32),
        compiler_params=_params("parallel", "arbitrary"),
    )(a, w, x)


def _ffn_body(x_ref, g_ref, w1_ref, w2_ref, fg_ref, o_ref, h_ref, *, final_norm):
    j = pl.program_id(1)

    @pl.when(j == 0)
    def _():
        _rmsnorm_rows(x_ref, g_ref, h_ref)
        o_ref[...] = x_ref[...]

    a = jnp.dot(h_ref[...], w1_ref[...], preferred_element_type=F32)
    a = jnp.square(jnp.maximum(a, 0.0)).astype(BF16)
    o_ref[...] += jnp.dot(a, w2_ref[...], preferred_element_type=F32)

    if final_norm:
        @pl.when(j == pl.num_programs(1) - 1)
        def _():
            _rmsnorm_rows(o_ref, fg_ref, o_ref)


def _ffn(x, g, w1, w2, layer, final_g, final_norm):
    m, d = x.shape
    f = w1.shape[2]
    tm, tf = TILES["ffn"]
    once = pl.Buffered(1)
    return pl.pallas_call(
        functools.partial(_ffn_body, final_norm=final_norm),
        name="ffn_final" if final_norm else "ffn",
        grid=(m // tm, f // tf),
        in_specs=[pl.BlockSpec((tm, d), lambda i, j: (i, 0), pipeline_mode=once),
                  pl.BlockSpec((1, d), lambda i, j: (0, 0)),
                  pl.BlockSpec((None, d, tf), lambda i, j: (layer, 0, j)),
                  pl.BlockSpec((None, tf, d), lambda i, j: (layer, j, 0)),
                  pl.BlockSpec((1, d), lambda i, j: (0, 0))],
        out_specs=pl.BlockSpec((tm, d), lambda i, j: (i, 0), pipeline_mode=once),
        out_shape=jax.ShapeDtypeStruct((m, d), F32),
        scratch_shapes=[pltpu.VMEM((tm, d), BF16)],
        compiler_params=_params("parallel", "arbitrary"),
    )(x, g.reshape(1, d), w1, w2, final_g.reshape(1, d))


def _conv_body(gc_ref, gp_ref, x_ref, wdw_ref, bdw_ref, lng_ref, lnb_ref, wout_ref, bout_ref,
               o_ref, ext_ref, y_ref, s_ref, *, seq_tiles, row_chunk):
    ts, d = gc_ref.shape
    kw = wdw_ref.shape[0]
    first = (pl.program_id(0) % seq_tiles) == 0
    lead = CONV_HALO - (kw - 1)

    for p in range(d // LANES):
        lanes = slice(p * LANES, (p + 1) * LANES)
        ext_ref[p, 0:CONV_HALO, :] = jnp.where(first, 0.0, gp_ref[:, lanes])
        ext_ref[p, CONV_HALO:, :] = gc_ref[:, lanes]
        w_rows = [wdw_ref[k:k + 1, lanes] for k in range(kw)]
        bias = jnp.broadcast_to(bdw_ref[:, lanes], (row_chunk, LANES))

        def row_body(rc, carry, p=p, lanes=lanes, w_rows=w_rows, bias=bias):
            base = pl.multiple_of(rc * row_chunk, row_chunk)
            acc = bias
            for k in range(kw):
                acc = acc + w_rows[k] * ext_ref[p, pl.ds(base + lead + k, row_chunk), :]
            y_ref[pl.ds(base, row_chunk), lanes] = acc
            return carry

        lax.fori_loop(0, ts // row_chunk, row_body, 0)

    lng = lng_ref[...]
    lnb = lnb_ref[...]
    rows = 256

    def ln_body(i, carry):
        r = pl.ds(pl.multiple_of(i * rows, rows), rows)
        y = y_ref[r, :]
        mu = jnp.mean(y, axis=-1, keepdims=True)
        yc = y - mu
        var = jnp.mean(yc * yc, axis=-1, keepdims=True)
        yn = yc * lax.rsqrt(var + EPS) * lng + lnb
        s_ref[r, :] = (yn * jax.nn.sigmoid(yn)).astype(BF16)
        return carry

    lax.fori_loop(0, ts // rows, ln_body, 0)
    o_ref[...] = (x_ref[...] + jnp.dot(s_ref[...], wout_ref[...], preferred_element_type=F32)
                  + bout_ref[...])


def _conv_tail(g, x, layer, w_dw, b_dw, ln_g, ln_b, w_out, b_out, seq):
    m, d = g.shape
    kw = w_dw.shape[1]
    ts = TILES["conv_rows"]
    assert kw - 1 <= CONV_HALO and seq % ts == 0 and ts % CONV_HALO == 0
    halo_blocks = ts // CONV_HALO
    row3 = lambda v: v.reshape(v.shape[0], 1, d)
    layer_row = pl.BlockSpec((None, 1, d), lambda i: (layer, 0, 0))
    return pl.pallas_call(
        functools.partial(_conv_body, seq_tiles=seq // ts, row_chunk=64),
        name="conv_tail",
        grid=(m // ts,),
        in_specs=[pl.BlockSpec((ts, d), lambda i: (i, 0)),
                  pl.BlockSpec((CONV_HALO, d), lambda i: (jnp.maximum(i * halo_blocks - 1, 0), 0)),
                  pl.BlockSpec((ts, d), lambda i: (i, 0)),
                  pl.BlockSpec((None, kw, d), lambda i: (layer, 0, 0)),
                  layer_row, layer_row, layer_row,
                  pl.BlockSpec((None, d, d), lambda i: (layer, 0, 0)),
                  layer_row],
        out_specs=pl.BlockSpec((ts, d), lambda i: (i, 0)),
        out_shape=jax.ShapeDtypeStruct((m, d), F32),
        scratch_shapes=[pltpu.VMEM((d // LANES, ts + CONV_HALO, LANES), F32),
                        pltpu.VMEM((ts, d), F32),
                        pltpu.VMEM((ts, d), BF16)],
        compiler_params=_params("arbitrary"),
    )(g, g, x, w_dw, row3(b_dw), row3(ln_g), row3(ln_b), w_out, row3(b_out))


def _mlstm_body(q_ref, k_ref, v_ref, so_ref, gate_ref, ng_ref, y_ref, c_ref, n_ref, m_ref):
    L = q_ref.shape[0]

    @pl.when(pl.program_id(2) == 0)
    def _():
        c_ref[...] = jnp.zeros(c_ref.shape, F32)
        n_ref[...] = jnp.zeros(n_ref.shape, F32)
        m_ref[...] = jnp.zeros(m_ref.shape, F32)

    qb = q_ref[...]
    kb = k_ref[...]
    vb = v_ref[...]

    ig_r = gate_ref[0:1, :]
    lf_r = jax.nn.log_sigmoid(gate_ref[1:2, :])
    t_idx = lax.broadcasted_iota(jnp.int32, (L, L), 0)
    s_idx = lax.broadcasted_iota(jnp.int32, (L, L), 1)
    causal = s_idx <= t_idx
    diag = s_idx == t_idx
    lf_c = jnp.sum(jnp.where(diag, lf_r, 0.0), axis=1, keepdims=True)
    ig_c = jnp.sum(jnp.where(diag, ig_r, 0.0), axis=1, keepdims=True)
    b_c = jnp.sum(jnp.where(causal, lf_r, 0.0), axis=1, keepdims=True)
    b_r = jnp.sum(jnp.where(t_idx <= s_idx, lf_c, 0.0), axis=0, keepdims=True)
    a_r = ig_r - b_r
    a_c = ig_c - b_c
    a_mat = jnp.where(causal, a_r, -jnp.inf)
    m_prev = m_ref[0:1, 0:1]
    g_c = jnp.maximum(m_prev, jnp.max(a_mat, axis=1, keepdims=True))
    d_mat = jnp.exp(a_mat - g_c)
    inter_c = jnp.exp(m_prev - g_c)
    m_c = b_c + g_c
    b_last = b_c[L - 1:L, :]
    m_new = m_c[L - 1:L, :]

    s = lax.dot_general(qb, kb, (((1,), (1,)), ((), ())), preferred_element_type=F32) * d_mat
    num = (jnp.dot(s.astype(BF16), vb, preferred_element_type=F32)
           + inter_c * jnp.dot(qb, c_ref[...].astype(BF16), preferred_element_type=F32))
    den = (jnp.sum(s, axis=1, keepdims=True)
           + inter_c * jnp.sum(qb.astype(F32) * n_ref[...], axis=1, keepdims=True))
    h = num / jnp.maximum(jnp.abs(den), jnp.exp(-m_c))

    w_c = jnp.exp(b_last + a_c - m_new)
    decay = jnp.exp(b_last + m_prev - m_new)
    c_ref[...] = decay * c_ref[...] + lax.dot_general(
        kb, (w_c * vb.astype(F32)).astype(BF16), (((0,), (0,)), ((), ())), preferred_element_type=F32)
    n_ref[...] = decay * n_ref[...] + jnp.sum(w_c * kb.astype(F32), axis=0, keepdims=True)
    m_ref[...] = jnp.broadcast_to(m_new, m_ref.shape)

    hn = h * lax.rsqrt(jnp.mean(h * h, axis=1, keepdims=True) + EPS) * ng_ref[...]
    y_ref[...] = (so_ref[...] * hn).astype(y_ref.dtype)


def _mlstm_recurrence(qk, v, sig_o, gates, norm_g, batch, seq):
    heads = MLSTM_HEADS
    L = MLSTM_CHUNK
    hqk = qk.shape[1] // 2
    hv = v.shape[1]
    dqk, dv = hqk // heads, hv // heads
    as3 = lambda t: t.reshape(batch, seq, t.shape[1])
    head_blk = lambda width: pl.BlockSpec((None, L, width), lambda b, h, c: (b, c, h))
    return pl.pallas_call(
        _mlstm_body,
        name="mlstm_recurrence",
        grid=(batch, heads, seq // L),
        in_specs=[head_blk(dqk),
                  pl.BlockSpec((None, L, dqk), lambda b, h, c: (b, c, heads + h)),
                  head_blk(dv),
                  head_blk(dv),
                  pl.BlockSpec((None, None, None, 2, L), lambda b, h, c: (b, h, c, 0, 0)),
                  pl.BlockSpec((1, dv), lambda b, h, c: (0, h))],
        out_specs=head_blk(dv),
        out_shape=jax.ShapeDtypeStruct((batch, seq, hv), BF16),
        scratch_shapes=[pltpu.VMEM((dqk, dv), F32),
                        pltpu.VMEM((1, dqk), F32),
                        pltpu.VMEM((8, LANES), F32)],
        compiler_params=_params("parallel", "parallel", "arbitrary"),
    )(as3(qk), as3(qk), as3(v), as3(sig_o), gates, norm_g.reshape(1, hv)).reshape(batch * seq, hv)


def _sgu_body(u_ref, v_ref, x_ref, ng_ref, ws_ref, bst_ref, wout_ref, bout_ref, o_ref,
              vn_ref, gated_ref):
    tm, width = u_ref.shape
    groups, chunk, _ = ws_ref.shape
    gw = width // groups
    _rmsnorm_rows(v_ref, ng_ref, vn_ref)
    t_idx = lax.broadcasted_iota(jnp.int32, (chunk, chunk), 0)
    s_idx = lax.broadcasted_iota(jnp.int32, (chunk, chunk), 1)
    tril = (s_idx <= t_idx).astype(F32)
    for g in range(groups):
        ws_g = (ws_ref[g] * tril).astype(BF16)
        bias_c = bst_ref[:, g:g + 1]
        cols = slice(g * gw, (g + 1) * gw)
        for c in range(tm // chunk):
            rows = slice(c * chunk, (c + 1) * chunk)
            mixed = jnp.dot(ws_g, vn_ref[rows, cols], preferred_element_type=F32) + bias_c
            gated_ref[rows, cols] = (u_ref[rows, cols] * mixed).astype(BF16)
    o_ref[...] = (x_ref[...] + jnp.dot(gated_ref[...], wout_ref[...], preferred_element_type=F32)
                  + bout_ref[...])


def _sgu_tail(p, x, layer, norm_g, w_s, b_s, w_out, b_out):
    m, d = x.shape
    width = norm_g.shape[1]
    _, groups, chunk, _ = w_s.shape
    tm = TILES["sgu_rows"]
    row3 = lambda v: v.reshape(v.shape[0], 1, v.shape[1])
    return pl.pallas_call(
        _sgu_body,
        name="sgu_tail",
        grid=(m // tm,),
        in_specs=[pl.BlockSpec((tm, width), lambda i: (i, 0)),
                  pl.BlockSpec((tm, width), lambda i: (i, 1)),
                  pl.BlockSpec((tm, d), lambda i: (i, 0)),
                  pl.BlockSpec((None, 1, width), lambda i: (layer, 0, 0)),
                  pl.BlockSpec((None, groups, chunk, chunk), lambda i: (layer, 0, 0, 0)),
                  pl.BlockSpec((None, chunk, groups), lambda i: (layer, 0, 0)),
                  pl.BlockSpec((None, width, d), lambda i: (layer, 0, 0)),
                  pl.BlockSpec((None, 1, d), lambda i: (layer, 0, 0))],
        out_specs=pl.BlockSpec((tm, d), lambda i: (i, 0)),
        out_shape=jax.ShapeDtypeStruct((m, d), F32),
        scratch_shapes=[pltpu.VMEM((tm, width), BF16), pltpu.VMEM((tm, width), BF16)],
        compiler_params=_params("parallel"),
    )(p, p, x, row3(norm_g), w_s, jnp.transpose(b_s, (0, 2, 1)), w_out, row3(b_out))


def _glu(val, gate):
    return val * jax.nn.sigmoid(gate)


def _identity(a):
    return a


def _sigmoid(a):
    return jax.nn.sigmoid(a)


def _gelu(a):
    return 0.5 * a * (1.0 + lax.erf(a * (2.0 ** -0.5)))


def _mlstm_layer(x, ng, w_in, b_in, layer, w_qkconv, b_qkconv, norm_g, w_out, batch, seq):
    heads = MLSTM_HEADS
    n_qk = w_qkconv.shape[1]
    hv = norm_g.shape[0]
    dqk = n_qk // 2 // heads
    qk = _qk_projection(x, ng, w_in, b_in, layer, w_qkconv, b_qkconv, n_qk // 2, dqk ** -0.5, seq)
    v = _norm_matmul(x, ng, w_in, b_in, layer, (n_qk,), hv, _identity, out_dtype=BF16)
    sig_o = _norm_matmul(x, ng, w_in, b_in, layer, (n_qk + hv,), hv, _sigmoid)
    n_main = n_qk + 2 * hv
    w_gate = jnp.pad(w_in[layer:layer + 1, :, n_main:], ((0, 0), (0, 0), (0, LANES - 2 * heads)))
    b_gate = jnp.pad(b_in[layer:layer + 1, n_main:], ((0, 0), (0, LANES - 2 * heads)))
    gates = _norm_matmul(x, ng, w_gate, b_gate, 0, (0,), LANES, _identity, tn=LANES)
    nc = seq // MLSTM_CHUNK
    gates = gates[:, :2 * heads].reshape(batch, nc, MLSTM_CHUNK, 2, heads)
    gates = jnp.transpose(gates, (0, 4, 1, 3, 2))
    y = _mlstm_recurrence(qk, v, sig_o, gates, norm_g, batch, seq)
    return _matmul_residual(y, w_out, layer, x)


def kernel(x, norm_mix_g, norm_ffn_g, final_g, conv_w_in, conv_b_in, conv_w_dw, conv_b_dw, conv_ln_g, conv_ln_b, conv_w_out, conv_b_out, mlstm_w_in, mlstm_b_in, mlstm_w_qkconv, mlstm_b_qkconv, mlstm_norm_g, mlstm_w_out, sgu_w_in, sgu_b_in, sgu_norm_g, sgu_w_s, sgu_b_s, sgu_w_out, sgu_b_out, ffn_w1, ffn_w2):
    batch, seq, d = x.shape
    depth = norm_mix_g.shape[0]
    conv_w_in, conv_w_out, mlstm_w_in, mlstm_w_out, sgu_w_in, sgu_w_out, ffn_w1, ffn_w2 = (
        w.astype(BF16) for w in (conv_w_in, conv_w_out, mlstm_w_in, mlstm_w_out, sgu_w_in, sgu_w_out,
                                 ffn_w1, ffn_w2))
    h = x.reshape(batch * seq, d)
    for i in range(depth):
        kind, j = i % N_MIXERS, i // N_MIXERS
        ng = norm_mix_g[i]
        if kind == 0:
            g = _norm_matmul(h, ng, conv_w_in, conv_b_in, j, (0, d), d, _glu)
            h = _conv_tail(g, h, j, conv_w_dw, conv_b_dw, conv_ln_g, conv_ln_b, conv_w_out, conv_b_out, seq)
        elif kind == 1:
            h = _mlstm_layer(h, ng, mlstm_w_in, mlstm_b_in, j, mlstm_w_qkconv[j], mlstm_b_qkconv[j],
                             mlstm_norm_g[j], mlstm_w_out, batch, seq)
        else:
            p = _norm_matmul(h, ng, sgu_w_in, sgu_b_in, j, (0,), sgu_w_in.shape[2], _gelu)
            h = _sgu_tail(p, h, j, sgu_norm_g, sgu_w_s, sgu_b_s, sgu_w_out, sgu_b_out)
        h = _ffn(h, norm_ffn_g[i], ffn_w1, ffn_w2, i, final_g, final_norm=(i == depth - 1))
    return h.reshape(batch, seq, d)
```

```python
import functools

import jax
import jax.numpy as jnp
from jax import lax
from jax.experimental import pallas as pl
from jax.experimental.pallas import tpu as pltpu

EPS = 1e-6
N_MIXERS = 3
MLSTM_HEADS = 8
MLSTM_CHUNK = 256
SGU_CHUNK = 128
SGU_GROUPS = 8
CONV_HALO = 32
QK_HALO = 16
LANES = 128
BF16_ROWS = 16
SUB_ROWS = 256

F32 = jnp.float32
BF16 = jnp.bfloat16

V7X_VMEM_BYTES = 64 * 1024 * 1024
VMEM_LIMIT = V7X_VMEM_BYTES - 8 * 1024 * 1024

TILES = dict(
    ffn=(512, 1024),
    proj=(1024, 1024),
    out=(1024, 512),
    conv_rows=512,
    sgu_rows=512,
)


def _params(*sem):
    return pltpu.CompilerParams(dimension_semantics=sem, vmem_limit_bytes=VMEM_LIMIT)


def _rmsnorm_rows(x_ref, g_ref, h_ref, dst_offset=0, rows=256):
    g = g_ref[...]
    rows = min(rows, x_ref.shape[0])

    def body(i, carry):
        start = pl.multiple_of(i * rows, rows)
        x = x_ref[pl.ds(start, rows), :]
        ms = jnp.mean(x * x, axis=-1, keepdims=True)
        h_ref[pl.ds(start + dst_offset, rows), :] = (x * lax.rsqrt(ms + EPS) * g).astype(h_ref.dtype)
        return carry

    lax.fori_loop(0, x_ref.shape[0] // rows, body, 0)


def _norm_mm_body(*refs, n_w, epilogue):
    x_ref, g_ref = refs[0], refs[1]
    w_refs = refs[2:2 + n_w]
    b_refs = refs[2 + n_w:2 + 2 * n_w]
    o_ref, h_ref = refs[2 + 2 * n_w], refs[3 + 2 * n_w]

    @pl.when(pl.program_id(1) == 0)
    def _():
        _rmsnorm_rows(x_ref, g_ref, h_ref)

    for r in range(o_ref.shape[0] // SUB_ROWS):
        rows = slice(r * SUB_ROWS, (r + 1) * SUB_ROWS)
        h = h_ref[rows, :]
        accs = [jnp.dot(h, w[...], preferred_element_type=F32) + b[...]
                for w, b in zip(w_refs, b_refs)]
        o_ref[rows, :] = epilogue(*accs).astype(o_ref.dtype)


def _norm_matmul(x, g, w, b, col_offsets, n_out, epilogue, out_dtype=F32, tn=None):
    m, k = x.shape
    tm, tn_default = TILES["proj"]
    tn = tn or tn_default
    n_w = len(col_offsets)
    col_map = lambda i, j, o: (0, 0, j + o)
    w_specs = [pl.BlockSpec((None, k, tn), functools.partial(col_map, o=off // tn)) for off in col_offsets]
    b_specs = [pl.BlockSpec((None, 1, tn), functools.partial(col_map, o=off // tn)) for off in col_offsets]
    b3 = b.reshape(1, 1, -1)
    return pl.pallas_call(
        functools.partial(_norm_mm_body, n_w=n_w, epilogue=epilogue),
        name="norm_matmul_" + epilogue.__name__.strip("_"),
        grid=(m // tm, n_out // tn),
        in_specs=[pl.BlockSpec((tm, k), lambda i, j: (i, 0)),
                  pl.BlockSpec((1, k), lambda i, j: (0, 0))] + w_specs + b_specs,
        out_specs=pl.BlockSpec((tm, tn), lambda i, j: (i, j)),
        out_shape=jax.ShapeDtypeStruct((m, n_out), out_dtype),
        scratch_shapes=[pltpu.VMEM((tm, k), BF16)],
        compiler_params=_params("parallel", "arbitrary"),
    )(x, g.reshape(1, k), *([w] * n_w), *([b3] * n_w))


def _qk_proj_body(x_ref, xh_ref, g_ref, w_ref, b_ref, cw_ref, cb_ref, o_ref, h_ref, p_ref,
                  *, seq_tiles, q_tiles, q_scale):
    i, j = pl.program_id(0), pl.program_id(1)
    tm, tn = o_ref.shape
    kw = cw_ref.shape[0]

    @pl.when(j == 0)
    def _():
        _rmsnorm_rows(xh_ref, g_ref, h_ref)
        _rmsnorm_rows(x_ref, g_ref, h_ref, dst_offset=QK_HALO)

    first = (i % seq_tiles) == 0
    scale = jnp.where(j < q_tiles, q_scale, 1.0)
    for r in range(tm // SUB_ROWS):
        lo = QK_HALO + r * SUB_ROWS
        src = 0 if r == 0 else lo
        acc = jnp.dot(h_ref[src:lo + SUB_ROWS, :], w_ref[...], preferred_element_type=F32) + b_ref[...]
        for p in range(tn // LANES):
            lanes = slice(p * LANES, (p + 1) * LANES)
            if r == 0:
                p_ref[p, 0:QK_HALO, :] = jnp.where(first, 0.0, acc[0:QK_HALO, lanes])
            p_ref[p, lo:lo + SUB_ROWS, :] = acc[lo - src:, lanes]
            y = cb_ref[:, lanes]
            for t in range(kw):
                start = lo - (kw - 1) + t
                y = y + cw_ref[t:t + 1, lanes] * p_ref[p, start:start + SUB_ROWS, :]
            o_ref[r * SUB_ROWS:(r + 1) * SUB_ROWS, lanes] = (y * jax.nn.sigmoid(y) * scale).astype(o_ref.dtype)


def _qk_projection(x, g, w, b, conv_w, conv_b, n_q, q_scale, seq):
    m, k = x.shape
    tm, tn = TILES["proj"]
    n_out = conv_w.shape[1]
    kw = conv_w.shape[0]
    assert kw - 1 <= QK_HALO and seq % tm == 0 and tm % QK_HALO == 0
    halo_blocks = tm // QK_HALO
    b3 = b.reshape(1, 1, -1)
    return pl.pallas_call(
        functools.partial(_qk_proj_body, seq_tiles=seq // tm, q_tiles=n_q // tn, q_scale=q_scale),
        name="mlstm_qk_projection",
        grid=(m // tm, n_out // tn),
        in_specs=[pl.BlockSpec((tm, k), lambda i, j: (i, 0)),
                  pl.BlockSpec((QK_HALO, k), lambda i, j: (jnp.maximum(i * halo_blocks - 1, 0), 0)),
                  pl.BlockSpec((1, k), lambda i, j: (0, 0)),
                  pl.BlockSpec((None, k, tn), lambda i, j: (0, 0, j)),
                  pl.BlockSpec((None, 1, tn), lambda i, j: (0, 0, j)),
                  pl.BlockSpec((kw, tn), lambda i, j: (0, j)),
                  pl.BlockSpec((1, tn), lambda i, j: (0, j))],
        out_specs=pl.BlockSpec((tm, tn), lambda i, j: (i, j)),
        out_shape=jax.ShapeDtypeStruct((m, n_out), BF16),
        scratch_shapes=[pltpu.VMEM((QK_HALO + tm, k), BF16),
                        pltpu.VMEM((tn // LANES, QK_HALO + tm, LANES), F32)],
        compiler_params=_params("parallel", "arbitrary"),
    )(x, x, g.reshape(1, k), w, b3, conv_w, conv_b.reshape(1, n_out))


def _mm_res_body(a_ref, w_ref, x_ref, o_ref):
    o_ref[...] = x_ref[...] + jnp.dot(a_ref[...], w_ref[...], preferred_element_type=F32)


def _matmul_residual(a, w, x):
    m, k = a.shape
    n = w.shape[2]
    tm, tn = TILES["out"]
    return pl.pallas_call(
        _mm_res_body,
        name="matmul_residual",
        grid=(m // tm, n // tn),
        in_specs=[pl.BlockSpec((tm, k), lambda i, j: (i, 0)),
                  pl.BlockSpec((None, k, tn), lambda i, j: (0, 0, j)),
                  pl.BlockSpec((tm, tn), lambda i, j: (i, j))],
        out_specs=pl.BlockSpec((tm, tn), lambda i, j: (i, j)),
        out_shape=jax.ShapeDtypeStruct((m, n), F32),
        compiler_params=_params("parallel", "arbitrary"),
    )(a, w, x)


def _ffn_body(*refs, n_cast, final_norm):
    x_ref, g_ref, w1_ref, w2_ref, fg_ref = refs[:5]
    src_refs = refs[5:5 + n_cast]
    o_ref = refs[5 + n_cast]
    dst_refs = refs[6 + n_cast:6 + 2 * n_cast]
    h_ref = refs[6 + 2 * n_cast]
    j = pl.program_id(1)

    @pl.when(j == 0)
    def _():
        _rmsnorm_rows(x_ref, g_ref, h_ref)
        o_ref[...] = x_ref[...]

    a = jnp.dot(h_ref[...], w1_ref[...], preferred_element_type=F32)
    a = jnp.square(jnp.maximum(a, 0.0)).astype(BF16)
    o_ref[...] += jnp.dot(a, w2_ref[...], preferred_element_type=F32)

    for src, dst in zip(src_refs, dst_refs):
        dst[...] = src[...].astype(BF16)

    if final_norm:
        @pl.when(j == pl.num_programs(1) - 1)
        def _():
            _rmsnorm_rows(o_ref, fg_ref, o_ref)


def _ffn(x, g, w1, w2, final_g, final_norm, cast_jobs=()):
    m, d = x.shape
    f = w1.shape[2]
    tm, tf = TILES["ffn"]
    nj = f // tf
    steps = (m // tm) * nj
    src_specs, dst_specs, dst_shapes = [], [], []
    for stack, l in cast_jobs:
        _, r, c = stack.shape
        rb = max(BF16_ROWS, r // steps)
        hold = steps // (r // rb)
        assert r % rb == 0 and steps % (r // rb) == 0
        src_specs.append(pl.BlockSpec((None, rb, c), functools.partial(
            lambda i, j, l, hold: (l, (i * nj + j) // hold, 0), l=l, hold=hold)))
        dst_specs.append(pl.BlockSpec((None, rb, c), functools.partial(
            lambda i, j, hold: (0, (i * nj + j) // hold, 0), hold=hold)))
        dst_shapes.append(jax.ShapeDtypeStruct((1, r, c), BF16))
    outs = pl.pallas_call(
        functools.partial(_ffn_body, n_cast=len(cast_jobs), final_norm=final_norm),
        name="ffn_final" if final_norm else "ffn",
        grid=(m // tm, nj),
        in_specs=[pl.BlockSpec((tm, d), lambda i, j: (i, 0)),
                  pl.BlockSpec((1, d), lambda i, j: (0, 0)),
                  pl.BlockSpec((None, d, tf), lambda i, j: (0, 0, j)),
                  pl.BlockSpec((None, tf, d), lambda i, j: (0, j, 0)),
                  pl.BlockSpec((1, d), lambda i, j: (0, 0))] + src_specs,
        out_specs=[pl.BlockSpec((tm, d), lambda i, j: (i, 0))] + dst_specs,
        out_shape=[jax.ShapeDtypeStruct((m, d), F32)] + dst_shapes,
        scratch_shapes=[pltpu.VMEM((tm, d), BF16)],
        compiler_params=_params("arbitrary", "arbitrary"),
    )(x, g.reshape(1, d), w1, w2, final_g.reshape(1, d), *[stack for stack, _ in cast_jobs])
    return outs[0], list(outs[1:])


def _conv_body(gc_ref, gp_ref, x_ref, wdw_ref, bdw_ref, lng_ref, lnb_ref, wout_ref, bout_ref,
               o_ref, ext_ref, y_ref, s_ref, *, seq_tiles, row_chunk):
    ts, d = gc_ref.shape
    kw = wdw_ref.shape[0]
    first = (pl.program_id(0) % seq_tiles) == 0
    lead = CONV_HALO - (kw - 1)

    for p in range(d // LANES):
        lanes = slice(p * LANES, (p + 1) * LANES)
        ext_ref[p, 0:CONV_HALO, :] = jnp.where(first, 0.0, gp_ref[:, lanes])
        ext_ref[p, CONV_HALO:, :] = gc_ref[:, lanes]
        w_rows = [wdw_ref[k:k + 1, lanes] for k in range(kw)]
        bias = jnp.broadcast_to(bdw_ref[:, lanes], (row_chunk, LANES))

        def row_body(rc, carry, p=p, lanes=lanes, w_rows=w_rows, bias=bias):
            base = pl.multiple_of(rc * row_chunk, row_chunk)
            acc = bias
            for k in range(kw):
                acc = acc + w_rows[k] * ext_ref[p, pl.ds(base + lead + k, row_chunk), :]
            y_ref[pl.ds(base, row_chunk), lanes] = acc
            return carry

        lax.fori_loop(0, ts // row_chunk, row_body, 0)

    lng = lng_ref[...]
    lnb = lnb_ref[...]
    rows = 256

    def ln_body(i, carry):
        r = pl.ds(pl.multiple_of(i * rows, rows), rows)
        y = y_ref[r, :]
        mu = jnp.mean(y, axis=-1, keepdims=True)
        yc = y - mu
        var = jnp.mean(yc * yc, axis=-1, keepdims=True)
        yn = yc * lax.rsqrt(var + EPS) * lng + lnb
        s_ref[r, :] = (yn * jax.nn.sigmoid(yn)).astype(BF16)
        return carry

    lax.fori_loop(0, ts // rows, ln_body, 0)
    o_ref[...] = (x_ref[...] + jnp.dot(s_ref[...], wout_ref[...], preferred_element_type=F32)
                  + bout_ref[...])


def _conv_tail(g, x, w_dw, b_dw, ln_g, ln_b, w_out, b_out, seq):
    m, d = g.shape
    kw = w_dw.shape[0]
    ts = TILES["conv_rows"]
    assert kw - 1 <= CONV_HALO and seq % ts == 0 and ts % CONV_HALO == 0
    halo_blocks = ts // CONV_HALO
    row = lambda v: v.reshape(1, d)
    const = lambda i: (0, 0)
    return pl.pallas_call(
        functools.partial(_conv_body, seq_tiles=seq // ts, row_chunk=64),
        name="conv_tail",
        grid=(m // ts,),
        in_specs=[pl.BlockSpec((ts, d), lambda i: (i, 0)),
                  pl.BlockSpec((CONV_HALO, d), lambda i: (jnp.maximum(i * halo_blocks - 1, 0), 0)),
                  pl.BlockSpec((ts, d), lambda i: (i, 0)),
                  pl.BlockSpec((kw, d), const),
                  pl.BlockSpec((1, d), const), pl.BlockSpec((1, d), const), pl.BlockSpec((1, d), const),
                  pl.BlockSpec((None, d, d), lambda i: (0, 0, 0)),
                  pl.BlockSpec((1, d), const)],
        out_specs=pl.BlockSpec((ts, d), lambda i: (i, 0)),
        out_shape=jax.ShapeDtypeStruct((m, d), F32),
        scratch_shapes=[pltpu.VMEM((d // LANES, ts + CONV_HALO, LANES), F32),
                        pltpu.VMEM((ts, d), F32),
                        pltpu.VMEM((ts, d), BF16)],
        compiler_params=_params("arbitrary"),
    )(g, g, x, w_dw, row(b_dw), row(ln_g), row(ln_b), w_out, row(b_out))


def _mlstm_body(q_ref, k_ref, v_ref, so_ref, gate_ref, ng_ref, y_ref, c_ref, n_ref, m_ref):
    L = q_ref.shape[0]

    @pl.when(pl.program_id(2) == 0)
    def _():
        c_ref[...] = jnp.zeros(c_ref.shape, F32)
        n_ref[...] = jnp.zeros(n_ref.shape, F32)
        m_ref[...] = jnp.zeros(m_ref.shape, F32)

    qb = q_ref[...]
    kb = k_ref[...]
    vb = v_ref[...]

    ig_r = gate_ref[0:1, :]
    lf_r = jax.nn.log_sigmoid(gate_ref[1:2, :])
    t_idx = lax.broadcasted_iota(jnp.int32, (L, L), 0)
    s_idx = lax.broadcasted_iota(jnp.int32, (L, L), 1)
    causal = s_idx <= t_idx
    diag = s_idx == t_idx
    lf_c = jnp.sum(jnp.where(diag, lf_r, 0.0), axis=1, keepdims=True)
    ig_c = jnp.sum(jnp.where(diag, ig_r, 0.0), axis=1, keepdims=True)
    b_c = jnp.sum(jnp.where(causal, lf_r, 0.0), axis=1, keepdims=True)
    b_r = jnp.sum(jnp.where(t_idx <= s_idx, lf_c, 0.0), axis=0, keepdims=True)
    a_r = ig_r - b_r
    a_c = ig_c - b_c
    a_mat = jnp.where(causal, a_r, -jnp.inf)
    m_prev = m_ref[0:1, 0:1]
    g_c = jnp.maximum(m_prev, jnp.max(a_mat, axis=1, keepdims=True))
    d_mat = jnp.exp(a_mat - g_c)
    inter_c = jnp.exp(m_prev - g_c)
    m_c = b_c + g_c
    b_last = b_c[L - 1:L, :]
    m_new = m_c[L - 1:L, :]

    s = lax.dot_general(qb, kb, (((1,), (1,)), ((), ())), preferred_element_type=F32) * d_mat
    num = (jnp.dot(s.astype(BF16), vb, preferred_element_type=F32)
           + inter_c * jnp.dot(qb, c_ref[...].astype(BF16), preferred_element_type=F32))
    den = (jnp.sum(s, axis=1, keepdims=True)
           + inter_c * jnp.sum(qb.astype(F32) * n_ref[...], axis=1, keepdims=True))
    h = num / jnp.maximum(jnp.abs(den), jnp.exp(-m_c))

    w_c = jnp.exp(b_last + a_c - m_new)
    decay = jnp.exp(b_last + m_prev - m_new)
    c_ref[...] = decay * c_ref[...] + lax.dot_general(
        kb, (w_c * vb.astype(F32)).astype(BF16), (((0,), (0,)), ((), ())), preferred_element_type=F32)
    n_ref[...] = decay * n_ref[...] + jnp.sum(w_c * kb.astype(F32), axis=0, keepdims=True)
    m_ref[...] = jnp.broadcast_to(m_new, m_ref.shape)

    hn = h * lax.rsqrt(jnp.mean(h * h, axis=1, keepdims=True) + EPS) * ng_ref[...]
    y_ref[...] = (so_ref[...] * hn).astype(y_ref.dtype)


def _mlstm_recurrence(qk, v, sig_o, gates, norm_g, batch, seq):
    heads = MLSTM_HEADS
    L = MLSTM_CHUNK
    hqk = qk.shape[1] // 2
    hv = v.shape[1]
    dqk, dv = hqk // heads, hv // heads
    as3 = lambda t: t.reshape(batch, seq, t.shape[1])
    head_blk = lambda width: pl.BlockSpec((None, L, width), lambda b, h, c: (b, c, h))
    return pl.pallas_call(
        _mlstm_body,
        name="mlstm_recurrence",
        grid=(batch, heads, seq // L),
        in_specs=[head_blk(dqk),
                  pl.BlockSpec((None, L, dqk), lambda b, h, c: (b, c, heads + h)),
                  head_blk(dv),
                  head_blk(dv),
                  pl.BlockSpec((None, None, None, 2, L), lambda b, h, c: (b, h, c, 0, 0)),
                  pl.BlockSpec((1, dv), lambda b, h, c: (0, h))],
        out_specs=head_blk(dv),
        out_shape=jax.ShapeDtypeStruct((batch, seq, hv), BF16),
        scratch_shapes=[pltpu.VMEM((dqk, dv), F32),
                        pltpu.VMEM((1, dqk), F32),
                        pltpu.VMEM((8, LANES), F32)],
        compiler_params=_params("parallel", "parallel", "arbitrary"),
    )(as3(qk), as3(qk), as3(v), as3(sig_o), gates, norm_g.reshape(1, hv)).reshape(batch * seq, hv)


def _sgu_body(u_ref, v_ref, x_ref, ng_ref, ws_ref, bst_ref, wout_ref, bout_ref, o_ref,
              vn_ref, gated_ref):
    tm, width = u_ref.shape
    groups, chunk, _ = ws_ref.shape
    gw = width // groups
    _rmsnorm_rows(v_ref, ng_ref, vn_ref)
    t_idx = lax.broadcasted_iota(jnp.int32, (chunk, chunk), 0)
    s_idx = lax.broadcasted_iota(jnp.int32, (chunk, chunk), 1)
    tril = (s_idx <= t_idx).astype(F32)
    for g in range(groups):
        ws_g = (ws_ref[g] * tril).astype(BF16)
        bias_c = bst_ref[:, g:g + 1]
        cols = slice(g * gw, (g + 1) * gw)
        for c in range(tm // chunk):
            rows = slice(c * chunk, (c + 1) * chunk)
            mixed = jnp.dot(ws_g, vn_ref[rows, cols], preferred_element_type=F32) + bias_c
            gated_ref[rows, cols] = (u_ref[rows, cols] * mixed).astype(BF16)
    o_ref[...] = (x_ref[...] + jnp.dot(gated_ref[...], wout_ref[...], preferred_element_type=F32)
                  + bout_ref[...])


def _sgu_tail(p, x, norm_g, w_s, b_s, w_out, b_out):
    m, d = x.shape
    width = norm_g.shape[0]
    groups, chunk, _ = w_s.shape
    tm = TILES["sgu_rows"]
    const2 = lambda i: (0, 0)
    return pl.pallas_call(
        _sgu_body,
        name="sgu_tail",
        grid=(m // tm,),
        in_specs=[pl.BlockSpec((tm, width), lambda i: (i, 0)),
                  pl.BlockSpec((tm, width), lambda i: (i, 1)),
                  pl.BlockSpec((tm, d), lambda i: (i, 0)),
                  pl.BlockSpec((1, width), const2),
                  pl.BlockSpec((groups, chunk, chunk), lambda i: (0, 0, 0)),
                  pl.BlockSpec((chunk, groups), const2),
                  pl.BlockSpec((None, width, d), lambda i: (0, 0, 0)),
                  pl.BlockSpec((1, d), const2)],
        out_specs=pl.BlockSpec((tm, d), lambda i: (i, 0)),
        out_shape=jax.ShapeDtypeStruct((m, d), F32),
        scratch_shapes=[pltpu.VMEM((tm, width), BF16), pltpu.VMEM((tm, width), BF16)],
        compiler_params=_params("parallel"),
    )(p, p, x, norm_g.reshape(1, width), w_s, jnp.transpose(b_s), w_out, b_out.reshape(1, d))


def _glu(val, gate):
    return val * jax.nn.sigmoid(gate)


def _identity(a):
    return a


def _sigmoid(a):
    return jax.nn.sigmoid(a)


def _gelu(a):
    return 0.5 * a * (1.0 + lax.erf(a * (2.0 ** -0.5)))


def _mlstm_layer(x, ng, w_in, b_in, w_in_f32, w_qkconv, b_qkconv, norm_g, w_out, batch, seq):
    heads = MLSTM_HEADS
    n_qk = w_qkconv.shape[1]
    hv = norm_g.shape[0]
    dqk = n_qk // 2 // heads
    qk = _qk_projection(x, ng, w_in, b_in, w_qkconv, b_qkconv, n_qk // 2, dqk ** -0.5, seq)
    v = _norm_matmul(x, ng, w_in, b_in, (n_qk,), hv, _identity, out_dtype=BF16)
    sig_o = _norm_matmul(x, ng, w_in, b_in, (n_qk + hv,), hv, _sigmoid)
    n_main = n_qk + 2 * hv
    w_gate = jnp.pad(w_in_f32[:, n_main:], ((0, 0), (0, LANES - 2 * heads))).astype(BF16)[None]
    b_gate = jnp.pad(b_in[n_main:], (0, LANES - 2 * heads))
    gates = _norm_matmul(x, ng, w_gate, b_gate, (0,), LANES, _identity, tn=LANES)
    nc = seq // MLSTM_CHUNK
    gates = gates[:, :2 * heads].reshape(batch, nc, MLSTM_CHUNK, 2, heads)
    gates = jnp.transpose(gates, (0, 4, 1, 3, 2))
    y = _mlstm_recurrence(qk, v, sig_o, gates, norm_g, batch, seq)
    return _matmul_residual(y, w_out, x)


def kernel(x, norm_mix_g, norm_ffn_g, final_g, conv_w_in, conv_b_in, conv_w_dw, conv_b_dw, conv_ln_g, conv_ln_b, conv_w_out, conv_b_out, mlstm_w_in, mlstm_b_in, mlstm_w_qkconv, mlstm_b_qkconv, mlstm_norm_g, mlstm_w_out, sgu_w_in, sgu_b_in, sgu_norm_g, sgu_w_s, sgu_b_s, sgu_w_out, sgu_b_out, ffn_w1, ffn_w2):
    batch, seq, d = x.shape
    depth = norm_mix_g.shape[0]
    mixer_weights = ((conv_w_in, conv_w_out), (mlstm_w_in, mlstm_w_out), (sgu_w_in, sgu_w_out))

    def layer_weight_jobs(i):
        return [(w, i // N_MIXERS) for w in mixer_weights[i % N_MIXERS]] + [(ffn_w1, i), (ffn_w2, i)]

    w_mix_in, w_mix_out, w1, w2 = (w[l:l + 1].astype(BF16) for w, l in layer_weight_jobs(0))
    h = x.reshape(batch * seq, d)
    for i in range(depth):
        kind, j = i % N_MIXERS, i // N_MIXERS
        ng = norm_mix_g[i]
        if kind == 0:
            g = _norm_matmul(h, ng, w_mix_in, conv_b_in[j], (0, d), d, _glu)
            h = _conv_tail(g, h, conv_w_dw[j], conv_b_dw[j], conv_ln_g[j], conv_ln_b[j], w_mix_out,
                           conv_b_out[j], seq)
        elif kind == 1:
            h = _mlstm_layer(h, ng, w_mix_in, mlstm_b_in[j], mlstm_w_in[j], mlstm_w_qkconv[j],
                             mlstm_b_qkconv[j], mlstm_norm_g[j], w_mix_out, batch, seq)
        else:
            p = _norm_matmul(h, ng, w_mix_in, sgu_b_in[j], (0,), sgu_w_in.shape[2], _gelu)
            h = _sgu_tail(p, h, sgu_norm_g[j], sgu_w_s[j], sgu_b_s[j], w_mix_out, sgu_b_out[j])
        last = i == depth - 1
        h, nxt = _ffn(h, norm_ffn_g[i], w1, w2, final_g, final_norm=last,
                      cast_jobs=() if last else layer_weight_jobs(i + 1))
        if not last:
            w_mix_in, w_mix_out, w1, w2 = nxt
    return h.reshape(batch, seq, d)
```

```python
import functools

import jax
import jax.numpy as jnp
from jax import lax
from jax.experimental import pallas as pl
from jax.experimental.pallas import tpu as pltpu

EPS = 1e-6
N_MIXERS = 3
MLSTM_HEADS = 8
MLSTM_CHUNK = 256
MLSTM_HEADS_PER_STEP = 2
SGU_CHUNK = 128
SGU_GROUPS = 8
CONV_HALO = 32
QK_HALO = 16
LANES = 128
BF16_ROWS = 16
SUB_ROWS = 256

F32 = jnp.float32
BF16 = jnp.bfloat16

V7X_VMEM_BYTES = 64 * 1024 * 1024
VMEM_LIMIT = V7X_VMEM_BYTES - 8 * 1024 * 1024

TILES = dict(
    ffn=(512, 1024),
    proj=(1024, 1024),
    out=(1024, 512),
    conv_rows=512,
    sgu_rows=512,
)


def _params(*sem):
    return pltpu.CompilerParams(dimension_semantics=sem, vmem_limit_bytes=VMEM_LIMIT)


def _rmsnorm_rows(x_ref, g_ref, h_ref, dst_offset=0, rows=256):
    g = g_ref[...]
    rows = min(rows, x_ref.shape[0])

    def body(i, carry):
        start = pl.multiple_of(i * rows, rows)
        x = x_ref[pl.ds(start, rows), :]
        ms = jnp.mean(x * x, axis=-1, keepdims=True)
        h_ref[pl.ds(start + dst_offset, rows), :] = (x * lax.rsqrt(ms + EPS) * g).astype(h_ref.dtype)
        return carry

    lax.fori_loop(0, x_ref.shape[0] // rows, body, 0)


def _norm_mm_body(*refs, n_w, epilogue):
    x_ref, g_ref = refs[0], refs[1]
    w_refs = refs[2:2 + n_w]
    b_refs = refs[2 + n_w:2 + 2 * n_w]
    o_ref, h_ref = refs[2 + 2 * n_w], refs[3 + 2 * n_w]

    @pl.when(pl.program_id(1) == 0)
    def _():
        _rmsnorm_rows(x_ref, g_ref, h_ref)

    for r in range(o_ref.shape[0] // SUB_ROWS):
        rows = slice(r * SUB_ROWS, (r + 1) * SUB_ROWS)
        h = h_ref[rows, :]
        accs = [jnp.dot(h, w[...], preferred_element_type=F32) + b[...]
                for w, b in zip(w_refs, b_refs)]
        o_ref[rows, :] = epilogue(*accs).astype(o_ref.dtype)


def _norm_matmul(x, g, w, b, col_offsets, n_out, epilogue, out_dtype=F32, tn=None):
    m, k = x.shape
    tm, tn_default = TILES["proj"]
    tn = tn or tn_default
    n_w = len(col_offsets)
    col_map = lambda i, j, o: (0, 0, j + o)
    w_specs = [pl.BlockSpec((None, k, tn), functools.partial(col_map, o=off // tn)) for off in col_offsets]
    b_specs = [pl.BlockSpec((None, 1, tn), functools.partial(col_map, o=off // tn)) for off in col_offsets]
    b3 = b.reshape(1, 1, -1)
    return pl.pallas_call(
        functools.partial(_norm_mm_body, n_w=n_w, epilogue=epilogue),
        name="norm_matmul_" + epilogue.__name__.strip("_"),
        grid=(m // tm, n_out // tn),
        in_specs=[pl.BlockSpec((tm, k), lambda i, j: (i, 0)),
                  pl.BlockSpec((1, k), lambda i, j: (0, 0))] + w_specs + b_specs,
        out_specs=pl.BlockSpec((tm, tn), lambda i, j: (i, j)),
        out_shape=jax.ShapeDtypeStruct((m, n_out), out_dtype),
        scratch_shapes=[pltpu.VMEM((tm, k), BF16)],
        compiler_params=_params("parallel", "arbitrary"),
    )(x, g.reshape(1, k), *([w] * n_w), *([b3] * n_w))


def _qk_proj_body(x_ref, xh_ref, g_ref, w_ref, b_ref, cw_ref, cb_ref, wg_ref, bg_ref, o_ref, og_ref,
                  h_ref, p_ref, *, seq_tiles, q_tiles, q_scale):
    i, j = pl.program_id(0), pl.program_id(1)
    tm, tn = o_ref.shape
    kw = cw_ref.shape[0]

    @pl.when(j == 0)
    def _():
        _rmsnorm_rows(xh_ref, g_ref, h_ref)
        _rmsnorm_rows(x_ref, g_ref, h_ref, dst_offset=QK_HALO)
        og_ref[...] = jnp.dot(h_ref[QK_HALO:, :], wg_ref[...], preferred_element_type=F32) + bg_ref[...]

    first = (i % seq_tiles) == 0
    scale = jnp.where(j < q_tiles, q_scale, 1.0)
    for r in range(tm // SUB_ROWS):
        lo = QK_HALO + r * SUB_ROWS
        src = 0 if r == 0 else lo
        acc = jnp.dot(h_ref[src:lo + SUB_ROWS, :], w_ref[...], preferred_element_type=F32) + b_ref[...]
        for p in range(tn // LANES):
            lanes = slice(p * LANES, (p + 1) * LANES)
            if r == 0:
                p_ref[p, 0:QK_HALO, :] = jnp.where(first, 0.0, acc[0:QK_HALO, lanes])
            p_ref[p, lo:lo + SUB_ROWS, :] = acc[lo - src:, lanes]
            y = cb_ref[:, lanes]
            for t in range(kw):
                start = lo - (kw - 1) + t
                y = y + cw_ref[t:t + 1, lanes] * p_ref[p, start:start + SUB_ROWS, :]
            o_ref[r * SUB_ROWS:(r + 1) * SUB_ROWS, lanes] = (y * jax.nn.sigmoid(y) * scale).astype(o_ref.dtype)


def _qk_projection(x, g, w, b, conv_w, conv_b, w_gate, b_gate, n_q, q_scale, seq):
    m, k = x.shape
    tm, tn = TILES["proj"]
    n_out = conv_w.shape[1]
    kw = conv_w.shape[0]
    assert kw - 1 <= QK_HALO and seq % tm == 0 and tm % QK_HALO == 0
    halo_blocks = tm // QK_HALO
    b3 = b.reshape(1, 1, -1)
    return pl.pallas_call(
        functools.partial(_qk_proj_body, seq_tiles=seq // tm, q_tiles=n_q // tn, q_scale=q_scale),
        name="mlstm_qk_projection",
        grid=(m // tm, n_out // tn),
        in_specs=[pl.BlockSpec((tm, k), lambda i, j: (i, 0)),
                  pl.BlockSpec((QK_HALO, k), lambda i, j: (jnp.maximum(i * halo_blocks - 1, 0), 0)),
                  pl.BlockSpec((1, k), lambda i, j: (0, 0)),
                  pl.BlockSpec((None, k, tn), lambda i, j: (0, 0, j)),
                  pl.BlockSpec((None, 1, tn), lambda i, j: (0, 0, j)),
                  pl.BlockSpec((kw, tn), lambda i, j: (0, j)),
                  pl.BlockSpec((1, tn), lambda i, j: (0, j)),
                  pl.BlockSpec((k, LANES), lambda i, j: (0, 0)),
                  pl.BlockSpec((1, LANES), lambda i, j: (0, 0))],
        out_specs=[pl.BlockSpec((tm, tn), lambda i, j: (i, j)),
                   pl.BlockSpec((tm, LANES), lambda i, j: (i, 0))],
        out_shape=[jax.ShapeDtypeStruct((m, n_out), BF16), jax.ShapeDtypeStruct((m, LANES), F32)],
        scratch_shapes=[pltpu.VMEM((QK_HALO + tm, k), BF16),
                        pltpu.VMEM((tn // LANES, QK_HALO + tm, LANES), F32)],
        compiler_params=_params("parallel", "arbitrary"),
    )(x, x, g.reshape(1, k), w, b3, conv_w, conv_b.reshape(1, n_out), w_gate, b_gate.reshape(1, LANES))


def _mm_res_body(a_ref, w_ref, x_ref, o_ref):
    o_ref[...] = x_ref[...] + jnp.dot(a_ref[...], w_ref[...], preferred_element_type=F32)


def _matmul_residual(a, w, x):
    m, k = a.shape
    n = w.shape[2]
    tm, tn = TILES["out"]
    return pl.pallas_call(
        _mm_res_body,
        name="matmul_residual",
        grid=(m // tm, n // tn),
        in_specs=[pl.BlockSpec((tm, k), lambda i, j: (i, 0)),
                  pl.BlockSpec((None, k, tn), lambda i, j: (0, 0, j)),
                  pl.BlockSpec((tm, tn), lambda i, j: (i, j))],
        out_specs=pl.BlockSpec((tm, tn), lambda i, j: (i, j)),
        out_shape=jax.ShapeDtypeStruct((m, n), F32),
        compiler_params=_params("parallel", "arbitrary"),
    )(a, w, x)


def _ffn_body(*refs, n_cast, final_norm):
    x_ref, g_ref, w1_ref, w2_ref, fg_ref = refs[:5]
    src_refs = refs[5:5 + n_cast]
    o_ref = refs[5 + n_cast]
    dst_refs = refs[6 + n_cast:6 + 2 * n_cast]
    h_ref = refs[6 + 2 * n_cast]
    j = pl.program_id(1)

    @pl.when(j == 0)
    def _():
        _rmsnorm_rows(x_ref, g_ref, h_ref)
        o_ref[...] = x_ref[...]

    a = jnp.dot(h_ref[...], w1_ref[...], preferred_element_type=F32)
    a = jnp.square(jnp.maximum(a, 0.0)).astype(BF16)
    o_ref[...] += jnp.dot(a, w2_ref[...], preferred_element_type=F32)

    for src, dst in zip(src_refs, dst_refs):
        dst[...] = src[...].astype(BF16)

    if final_norm:
        @pl.when(j == pl.num_programs(1) - 1)
        def _():
            _rmsnorm_rows(o_ref, fg_ref, o_ref)


def _ffn(x, g, w1, w2, final_g, final_norm, cast_jobs=()):
    m, d = x.shape
    f = w1.shape[2]
    tm, tf = TILES["ffn"]
    nj = f // tf
    steps = (m // tm) * nj
    src_specs, dst_specs, dst_shapes = [], [], []
    for stack, l in cast_jobs:
        _, r, c = stack.shape
        rb = max(BF16_ROWS, r // steps)
        hold = steps // (r // rb)
        assert r % rb == 0 and steps % (r // rb) == 0
        src_specs.append(pl.BlockSpec((None, rb, c), functools.partial(
            lambda i, j, l, hold: (l, (i * nj + j) // hold, 0), l=l, hold=hold)))
        dst_specs.append(pl.BlockSpec((None, rb, c), functools.partial(
            lambda i, j, hold: (0, (i * nj + j) // hold, 0), hold=hold)))
        dst_shapes.append(jax.ShapeDtypeStruct((1, r, c), BF16))
    outs = pl.pallas_call(
        functools.partial(_ffn_body, n_cast=len(cast_jobs), final_norm=final_norm),
        name="ffn_final" if final_norm else "ffn",
        grid=(m // tm, nj),
        in_specs=[pl.BlockSpec((tm, d), lambda i, j: (i, 0)),
                  pl.BlockSpec((1, d), lambda i, j: (0, 0)),
                  pl.BlockSpec((None, d, tf), lambda i, j: (0, 0, j)),
                  pl.BlockSpec((None, tf, d), lambda i, j: (0, j, 0)),
                  pl.BlockSpec((1, d), lambda i, j: (0, 0))] + src_specs,
        out_specs=[pl.BlockSpec((tm, d), lambda i, j: (i, 0))] + dst_specs,
        out_shape=[jax.ShapeDtypeStruct((m, d), F32)] + dst_shapes,
        scratch_shapes=[pltpu.VMEM((tm, d), BF16)],
        compiler_params=_params("arbitrary", "arbitrary"),
    )(x, g.reshape(1, d), w1, w2, final_g.reshape(1, d), *[stack for stack, _ in cast_jobs])
    return outs[0], list(outs[1:])


def _conv_body(gc_ref, gp_ref, x_ref, wdw_ref, bdw_ref, lng_ref, lnb_ref, wout_ref, bout_ref,
               o_ref, ext_ref, y_ref, s_ref, *, seq_tiles, row_chunk):
    ts, d = gc_ref.shape
    kw = wdw_ref.shape[0]
    first = (pl.program_id(0) % seq_tiles) == 0
    lead = CONV_HALO - (kw - 1)

    for p in range(d // LANES):
        lanes = slice(p * LANES, (p + 1) * LANES)
        ext_ref[p, 0:CONV_HALO, :] = jnp.where(first, 0.0, gp_ref[:, lanes])
        ext_ref[p, CONV_HALO:, :] = gc_ref[:, lanes]
        w_rows = [wdw_ref[k:k + 1, lanes] for k in range(kw)]
        bias = jnp.broadcast_to(bdw_ref[:, lanes], (row_chunk, LANES))

        def row_body(rc, carry, p=p, lanes=lanes, w_rows=w_rows, bias=bias):
            base = pl.multiple_of(rc * row_chunk, row_chunk)
            acc = bias
            for k in range(kw):
                acc = acc + w_rows[k] * ext_ref[p, pl.ds(base + lead + k, row_chunk), :]
            y_ref[pl.ds(base, row_chunk), lanes] = acc
            return carry

        lax.fori_loop(0, ts // row_chunk, row_body, 0)

    lng = lng_ref[...]
    lnb = lnb_ref[...]
    rows = 256

    def ln_body(i, carry):
        r = pl.ds(pl.multiple_of(i * rows, rows), rows)
        y = y_ref[r, :]
        mu = jnp.mean(y, axis=-1, keepdims=True)
        yc = y - mu
        var = jnp.mean(yc * yc, axis=-1, keepdims=True)
        yn = yc * lax.rsqrt(var + EPS) * lng + lnb
        s_ref[r, :] = (yn * jax.nn.sigmoid(yn)).astype(BF16)
        return carry

    lax.fori_loop(0, ts // rows, ln_body, 0)
    o_ref[...] = (x_ref[...] + jnp.dot(s_ref[...], wout_ref[...], preferred_element_type=F32)
                  + bout_ref[...])


def _conv_tail(g, x, w_dw, b_dw, ln_g, ln_b, w_out, b_out, seq):
    m, d = g.shape
    kw = w_dw.shape[0]
    ts = TILES["conv_rows"]
    assert kw - 1 <= CONV_HALO and seq % ts == 0 and ts % CONV_HALO == 0
    halo_blocks = ts // CONV_HALO
    row = lambda v: v.reshape(1, d)
    const = lambda i: (0, 0)
    return pl.pallas_call(
        functools.partial(_conv_body, seq_tiles=seq // ts, row_chunk=64),
        name="conv_tail",
        grid=(m // ts,),
        in_specs=[pl.BlockSpec((ts, d), lambda i: (i, 0)),
                  pl.BlockSpec((CONV_HALO, d), lambda i: (jnp.maximum(i * halo_blocks - 1, 0), 0)),
                  pl.BlockSpec((ts, d), lambda i: (i, 0)),
                  pl.BlockSpec((kw, d), const),
                  pl.BlockSpec((1, d), const), pl.BlockSpec((1, d), const), pl.BlockSpec((1, d), const),
                  pl.BlockSpec((None, d, d), lambda i: (0, 0, 0)),
                  pl.BlockSpec((1, d), const)],
        out_specs=pl.BlockSpec((ts, d), lambda i: (i, 0)),
        out_shape=jax.ShapeDtypeStruct((m, d), F32),
        scratch_shapes=[pltpu.VMEM((d // LANES, ts + CONV_HALO, LANES), F32),
                        pltpu.VMEM((ts, d), F32),
                        pltpu.VMEM((ts, d), BF16)],
        compiler_params=_params("arbitrary"),
    )(g, g, x, w_dw, row(b_dw), row(ln_g), row(ln_b), w_out, row(b_out))


def _mlstm_body(q_ref, k_ref, v_ref, so_ref, gate_ref, ng_ref, y_ref, c_ref, n_ref, m_ref):
    L = q_ref.shape[0]
    n_heads = gate_ref.shape[0]
    dqk = q_ref.shape[1] // n_heads
    dv = v_ref.shape[1] // n_heads

    @pl.when(pl.program_id(2) == 0)
    def _():
        c_ref[...] = jnp.zeros(c_ref.shape, F32)
        n_ref[...] = jnp.zeros(n_ref.shape, F32)
        m_ref[...] = jnp.zeros(m_ref.shape, F32)

    t_idx = lax.broadcasted_iota(jnp.int32, (L, L), 0)
    s_idx = lax.broadcasted_iota(jnp.int32, (L, L), 1)
    causal = s_idx <= t_idx
    diag = s_idx == t_idx

    for hd in range(n_heads):
        qk_cols = slice(hd * dqk, (hd + 1) * dqk)
        v_cols = slice(hd * dv, (hd + 1) * dv)
        qb = q_ref[:, qk_cols]
        kb = k_ref[:, qk_cols]
        vb = v_ref[:, v_cols]

        ig_r = gate_ref[hd, 0:1, :]
        lf_r = jax.nn.log_sigmoid(gate_ref[hd, 1:2, :])
        lf_c = jnp.sum(jnp.where(diag, lf_r, 0.0), axis=1, keepdims=True)
        ig_c = jnp.sum(jnp.where(diag, ig_r, 0.0), axis=1, keepdims=True)
        b_c = jnp.sum(jnp.where(causal, lf_r, 0.0), axis=1, keepdims=True)
        b_r = jnp.sum(jnp.where(t_idx <= s_idx, lf_c, 0.0), axis=0, keepdims=True)
        a_r = ig_r - b_r
        a_c = ig_c - b_c
        a_mat = jnp.where(causal, a_r, -jnp.inf)
        m_prev = m_ref[hd, 0:1, 0:1]
        g_c = jnp.maximum(m_prev, jnp.max(a_mat, axis=1, keepdims=True))
        d_mat = jnp.exp(a_mat - g_c)
        inter_c = jnp.exp(m_prev - g_c)
        m_c = b_c + g_c
        b_last = b_c[L - 1:L, :]
        m_new = m_c[L - 1:L, :]

        s = lax.dot_general(qb, kb, (((1,), (1,)), ((), ())), preferred_element_type=F32) * d_mat
        num = (jnp.dot(s.astype(BF16), vb, preferred_element_type=F32)
               + inter_c * jnp.dot(qb, c_ref[hd].astype(BF16), preferred_element_type=F32))
        den = (jnp.sum(s, axis=1, keepdims=True)
               + inter_c * jnp.sum(qb.astype(F32) * n_ref[hd], axis=1, keepdims=True))
        h = num / jnp.maximum(jnp.abs(den), jnp.exp(-m_c))

        w_c = jnp.exp(b_last + a_c - m_new)
        decay = jnp.exp(b_last + m_prev - m_new)
        c_ref[hd] = decay * c_ref[hd] + lax.dot_general(
            kb, (w_c * vb.astype(F32)).astype(BF16), (((0,), (0,)), ((), ())), preferred_element_type=F32)
        n_ref[hd] = decay * n_ref[hd] + jnp.sum(w_c * kb.astype(F32), axis=0, keepdims=True)
        m_ref[hd] = jnp.broadcast_to(m_new, m_ref.shape[1:])

        hn = h * lax.rsqrt(jnp.mean(h * h, axis=1, keepdims=True) + EPS) * ng_ref[:, v_cols]
        y_ref[:, v_cols] = (so_ref[:, v_cols] * hn).astype(y_ref.dtype)


def _mlstm_recurrence(qk, v, sig_o, gates, norm_g, batch, seq):
    heads = MLSTM_HEADS
    hp = MLSTM_HEADS_PER_STEP
    L = MLSTM_CHUNK
    hqk = qk.shape[1] // 2
    hv = v.shape[1]
    dqk, dv = hqk // heads, hv // heads
    groups = heads // hp
    as3 = lambda t: t.reshape(batch, seq, t.shape[1])
    head_blk = lambda width: pl.BlockSpec((None, L, hp * width), lambda b, h, c: (b, c, h))
    return pl.pallas_call(
        _mlstm_body,
        name="mlstm_recurrence",
        grid=(batch, groups, seq // L),
        in_specs=[head_blk(dqk),
                  pl.BlockSpec((None, L, hp * dqk), lambda b, h, c: (b, c, groups + h)),
                  head_blk(dv),
                  head_blk(dv),
                  pl.BlockSpec((None, hp, None, 2, L), lambda b, h, c: (b, h, c, 0, 0)),
                  pl.BlockSpec((1, hp * dv), lambda b, h, c: (0, h))],
        out_specs=head_blk(dv),
        out_shape=jax.ShapeDtypeStruct((batch, seq, hv), BF16),
        scratch_shapes=[pltpu.VMEM((hp, dqk, dv), F32),
                        pltpu.VMEM((hp, 1, dqk), F32),
                        pltpu.VMEM((hp, 8, LANES), F32)],
        compiler_params=_params("parallel", "parallel", "arbitrary"),
    )(as3(qk), as3(qk), as3(v), as3(sig_o), gates, norm_g.reshape(1, hv)).reshape(batch * seq, hv)


def _sgu_body(u_ref, v_ref, x_ref, ng_ref, ws_ref, bst_ref, wout_ref, bout_ref, o_ref,
              vn_ref, gated_ref):
    tm, width = u_ref.shape
    groups, chunk, _ = ws_ref.shape
    gw = width // groups
    _rmsnorm_rows(v_ref, ng_ref, vn_ref)
    t_idx = lax.broadcasted_iota(jnp.int32, (chunk, chunk), 0)
    s_idx = lax.broadcasted_iota(jnp.int32, (chunk, chunk), 1)
    tril = (s_idx <= t_idx).astype(F32)
    for g in range(groups):
        ws_g = (ws_ref[g] * tril).astype(BF16)
        bias_c = bst_ref[:, g:g + 1]
        cols = slice(g * gw, (g + 1) * gw)
        for c in range(tm // chunk):
            rows = slice(c * chunk, (c + 1) * chunk)
            mixed = jnp.dot(ws_g, vn_ref[rows, cols], preferred_element_type=F32) + bias_c
            gated_ref[rows, cols] = (u_ref[rows, cols] * mixed).astype(BF16)
    o_ref[...] = (x_ref[...] + jnp.dot(gated_ref[...], wout_ref[...], preferred_element_type=F32)
                  + bout_ref[...])


def _sgu_tail(p, x, norm_g, w_s, b_s, w_out, b_out):
    m, d = x.shape
    width = norm_g.shape[0]
    groups, chunk, _ = w_s.shape
    tm = TILES["sgu_rows"]
    const2 = lambda i: (0, 0)
    return pl.pallas_call(
        _sgu_body,
        name="sgu_tail",
        grid=(m // tm,),
        in_specs=[pl.BlockSpec((tm, width), lambda i: (i, 0)),
                  pl.BlockSpec((tm, width), lambda i: (i, 1)),
                  pl.BlockSpec((tm, d), lambda i: (i, 0)),
                  pl.BlockSpec((1, width), const2),
                  pl.BlockSpec((groups, chunk, chunk), lambda i: (0, 0, 0)),
                  pl.BlockSpec((chunk, groups), const2),
                  pl.BlockSpec((None, width, d), lambda i: (0, 0, 0)),
                  pl.BlockSpec((1, d), const2)],
        out_specs=pl.BlockSpec((tm, d), lambda i: (i, 0)),
        out_shape=jax.ShapeDtypeStruct((m, d), F32),
        scratch_shapes=[pltpu.VMEM((tm, width), BF16), pltpu.VMEM((tm, width), BF16)],
        compiler_params=_params("parallel"),
    )(p, p, x, norm_g.reshape(1, width), w_s, jnp.transpose(b_s), w_out, b_out.reshape(1, d))


def _glu(val, gate):
    return val * jax.nn.sigmoid(gate)


def _identity(a):
    return a


def _sigmoid(a):
    return jax.nn.sigmoid(a)


def _gelu(a):
    return 0.5 * a * (1.0 + lax.erf(a * (2.0 ** -0.5)))


def _mlstm_layer(x, ng, w_in, b_in, w_qkconv, b_qkconv, norm_g, w_out, batch, seq):
    heads = MLSTM_HEADS
    n_qk = w_qkconv.shape[1]
    hv = norm_g.shape[0]
    dqk = n_qk // 2 // heads
    n_main = n_qk + 2 * hv
    w_gate = jnp.pad(w_in[0, :, n_main:], ((0, 0), (0, LANES - 2 * heads)))
    b_gate = jnp.pad(b_in[n_main:], (0, LANES - 2 * heads))
    qk, gates = _qk_projection(x, ng, w_in, b_in, w_qkconv, b_qkconv, w_gate, b_gate, n_qk // 2,
                               dqk ** -0.5, seq)
    v = _norm_matmul(x, ng, w_in, b_in, (n_qk,), hv, _identity, out_dtype=BF16)
    sig_o = _norm_matmul(x, ng, w_in, b_in, (n_qk + hv,), hv, _sigmoid)
    nc = seq // MLSTM_CHUNK
    gates = gates[:, :2 * heads].reshape(batch, nc, MLSTM_CHUNK, 2, heads)
    gates = jnp.transpose(gates, (0, 4, 1, 3, 2))
    y = _mlstm_recurrence(qk, v, sig_o, gates, norm_g, batch, seq)
    return _matmul_residual(y, w_out, x)


def kernel(x, norm_mix_g, norm_ffn_g, final_g, conv_w_in, conv_b_in, conv_w_dw, conv_b_dw, conv_ln_g, conv_ln_b, conv_w_out, conv_b_out, mlstm_w_in, mlstm_b_in, mlstm_w_qkconv, mlstm_b_qkconv, mlstm_norm_g, mlstm_w_out, sgu_w_in, sgu_b_in, sgu_norm_g, sgu_w_s, sgu_b_s, sgu_w_out, sgu_b_out, ffn_w1, ffn_w2):
    batch, seq, d = x.shape
    depth = norm_mix_g.shape[0]
    mixer_weights = ((conv_w_in, conv_w_out), (mlstm_w_in, mlstm_w_out), (sgu_w_in, sgu_w_out))

    def layer_weight_jobs(i):
        return [(w, i // N_MIXERS) for w in mixer_weights[i % N_MIXERS]] + [(ffn_w1, i), (ffn_w2, i)]

    w_mix_in, w_mix_out, w1, w2 = (w[l:l + 1].astype(BF16) for w, l in layer_weight_jobs(0))
    h = x.reshape(batch * seq, d)
    for i in range(depth):
        kind, j = i % N_MIXERS, i // N_MIXERS
        ng = norm_mix_g[i]
        if kind == 0:
            g = _norm_matmul(h, ng, w_mix_in, conv_b_in[j], (0, d), d, _glu)
            h = _conv_tail(g, h, conv_w_dw[j], conv_b_dw[j], conv_ln_g[j], conv_ln_b[j], w_mix_out,
                           conv_b_out[j], seq)
        elif kind == 1:
            h = _mlstm_layer(h, ng, w_mix_in, mlstm_b_in[j], mlstm_w_qkconv[j], mlstm_b_qkconv[j],
                             mlstm_norm_g[j], w_mix_out, batch, seq)
        else:
            p = _norm_matmul(h, ng, w_mix_in, sgu_b_in[j], (0,), sgu_w_in.shape[2], _gelu)
            h = _sgu_tail(p, h, sgu_norm_g[j], sgu_w_s[j], sgu_b_s[j], w_mix_out, sgu_b_out[j])
        last = i == depth - 1
        jobs = [] if last else layer_weight_jobs(i + 1)
        inside = [n for n, (w, _) in enumerate(jobs) if w.shape[2] % LANES == 0]
        h, cast = _ffn(h, norm_ffn_g[i], w1, w2, final_g, final_norm=last,
                       cast_jobs=[jobs[n] for n in inside])
        if not last:
            nxt = {n: c for n, c in zip(inside, cast)}
            w_mix_in, w_mix_out, w1, w2 = (nxt[n] if n in nxt else w[l:l + 1].astype(BF16)
                                           for n, (w, l) in enumerate(jobs))
    return h.reshape(batch, seq, d)
```

```python
import functools

import jax
import jax.numpy as jnp
from jax import lax
from jax.experimental import pallas as pl
from jax.experimental.pallas import tpu as pltpu

EPS = 1e-6
N_MIXERS = 3
MLSTM_HEADS = 8
MLSTM_CHUNK = 256
MLSTM_HEADS_PER_STEP = 2
SGU_CHUNK = 128
SGU_GROUPS = 8
CONV_HALO = 32
QK_HALO = 16
LANES = 128
BF16_ROWS = 16
SUB_ROWS = 256

F32 = jnp.float32
BF16 = jnp.bfloat16

V7X_VMEM_BYTES = 64 * 1024 * 1024
VMEM_LIMIT = V7X_VMEM_BYTES - 8 * 1024 * 1024

TILES = dict(
    ffn=(1024, 1024),
    proj=(1024, 1024),
    out=(1024, 512),
    conv_rows=512,
    sgu_rows=512,
)


def _params(*sem):
    return pltpu.CompilerParams(dimension_semantics=sem, vmem_limit_bytes=VMEM_LIMIT)


def _rmsnorm_rows(x_ref, g_ref, h_ref, dst_offset=0, rows=256):
    g = g_ref[...]
    rows = min(rows, x_ref.shape[0])

    def body(i, carry):
        start = pl.multiple_of(i * rows, rows)
        x = x_ref[pl.ds(start, rows), :]
        ms = jnp.mean(x * x, axis=-1, keepdims=True)
        h_ref[pl.ds(start + dst_offset, rows), :] = (x * lax.rsqrt(ms + EPS) * g).astype(h_ref.dtype)
        return carry

    lax.fori_loop(0, x_ref.shape[0] // rows, body, 0)


def _norm_mm_body(*refs, n_w, epilogue):
    x_ref, g_ref = refs[0], refs[1]
    w_refs = refs[2:2 + n_w]
    b_refs = refs[2 + n_w:2 + 2 * n_w]
    o_ref, h_ref = refs[2 + 2 * n_w], refs[3 + 2 * n_w]

    @pl.when(pl.program_id(1) == 0)
    def _():
        _rmsnorm_rows(x_ref, g_ref, h_ref)

    for r in range(o_ref.shape[0] // SUB_ROWS):
        rows = slice(r * SUB_ROWS, (r + 1) * SUB_ROWS)
        h = h_ref[rows, :]
        accs = [jnp.dot(h, w[...], preferred_element_type=F32) + b[...]
                for w, b in zip(w_refs, b_refs)]
        o_ref[rows, :] = epilogue(*accs).astype(o_ref.dtype)


def _norm_matmul(x, g, w, b, col_offsets, n_out, epilogue, out_dtype=F32, tn=None):
    m, k = x.shape
    tm, tn_default = TILES["proj"]
    tn = tn or tn_default
    n_w = len(col_offsets)
    col_map = lambda i, j, o: (0, 0, j + o)
    w_specs = [pl.BlockSpec((None, k, tn), functools.partial(col_map, o=off // tn)) for off in col_offsets]
    b_specs = [pl.BlockSpec((None, 1, tn), functools.partial(col_map, o=off // tn)) for off in col_offsets]
    b3 = b.reshape(1, 1, -1)
    return pl.pallas_call(
        functools.partial(_norm_mm_body, n_w=n_w, epilogue=epilogue),
        name="norm_matmul_" + epilogue.__name__.strip("_"),
        grid=(m // tm, n_out // tn),
        in_specs=[pl.BlockSpec((tm, k), lambda i, j: (i, 0)),
                  pl.BlockSpec((1, k), lambda i, j: (0, 0))] + w_specs + b_specs,
        out_specs=pl.BlockSpec((tm, tn), lambda i, j: (i, j)),
        out_shape=jax.ShapeDtypeStruct((m, n_out), out_dtype),
        scratch_shapes=[pltpu.VMEM((tm, k), BF16)],
        compiler_params=_params("parallel", "arbitrary"),
    )(x, g.reshape(1, k), *([w] * n_w), *([b3] * n_w))


def _qk_proj_body(x_ref, xh_ref, g_ref, w_ref, b_ref, cw_ref, cb_ref, wg_ref, bg_ref, o_ref, og_ref,
                  h_ref, p_ref, *, seq_tiles, q_tiles, q_scale):
    i, j = pl.program_id(0), pl.program_id(1)
    tm, tn = o_ref.shape
    kw = cw_ref.shape[0]

    @pl.when(j == 0)
    def _():
        _rmsnorm_rows(xh_ref, g_ref, h_ref)
        _rmsnorm_rows(x_ref, g_ref, h_ref, dst_offset=QK_HALO)
        og_ref[...] = jnp.dot(h_ref[QK_HALO:, :], wg_ref[...], preferred_element_type=F32) + bg_ref[...]

    first = (i % seq_tiles) == 0
    scale = jnp.where(j < q_tiles, q_scale, 1.0)
    for r in range(tm // SUB_ROWS):
        lo = QK_HALO + r * SUB_ROWS
        src = 0 if r == 0 else lo
        acc = jnp.dot(h_ref[src:lo + SUB_ROWS, :], w_ref[...], preferred_element_type=F32) + b_ref[...]
        for p in range(tn // LANES):
            lanes = slice(p * LANES, (p + 1) * LANES)
            if r == 0:
                p_ref[p, 0:QK_HALO, :] = jnp.where(first, 0.0, acc[0:QK_HALO, lanes])
            p_ref[p, lo:lo + SUB_ROWS, :] = acc[lo - src:, lanes]
            y = cb_ref[:, lanes]
            for t in range(kw):
                start = lo - (kw - 1) + t
                y = y + cw_ref[t:t + 1, lanes] * p_ref[p, start:start + SUB_ROWS, :]
            o_ref[r * SUB_ROWS:(r + 1) * SUB_ROWS, lanes] = (y * jax.nn.sigmoid(y) * scale).astype(o_ref.dtype)


def _qk_projection(x, g, w, b, conv_w, conv_b, w_gate, b_gate, n_q, q_scale, seq):
    m, k = x.shape
    tm, tn = TILES["proj"]
    n_out = conv_w.shape[1]
    kw = conv_w.shape[0]
    assert kw - 1 <= QK_HALO and seq % tm == 0 and tm % QK_HALO == 0
    halo_blocks = tm // QK_HALO
    b3 = b.reshape(1, 1, -1)
    return pl.pallas_call(
        functools.partial(_qk_proj_body, seq_tiles=seq // tm, q_tiles=n_q // tn, q_scale=q_scale),
        name="mlstm_qk_projection",
        grid=(m // tm, n_out // tn),
        in_specs=[pl.BlockSpec((tm, k), lambda i, j: (i, 0)),
                  pl.BlockSpec((QK_HALO, k), lambda i, j: (jnp.maximum(i * halo_blocks - 1, 0), 0)),
                  pl.BlockSpec((1, k), lambda i, j: (0, 0)),
                  pl.BlockSpec((None, k, tn), lambda i, j: (0, 0, j)),
                  pl.BlockSpec((None, 1, tn), lambda i, j: (0, 0, j)),
                  pl.BlockSpec((kw, tn), lambda i, j: (0, j)),
                  pl.BlockSpec((1, tn), lambda i, j: (0, j)),
                  pl.BlockSpec((k, LANES), lambda i, j: (0, 0)),
                  pl.BlockSpec((1, LANES), lambda i, j: (0, 0))],
        out_specs=[pl.BlockSpec((tm, tn), lambda i, j: (i, j)),
                   pl.BlockSpec((tm, LANES), lambda i, j: (i, 0))],
        out_shape=[jax.ShapeDtypeStruct((m, n_out), BF16), jax.ShapeDtypeStruct((m, LANES), F32)],
        scratch_shapes=[pltpu.VMEM((QK_HALO + tm, k), BF16),
                        pltpu.VMEM((tn // LANES, QK_HALO + tm, LANES), F32)],
        compiler_params=_params("parallel", "arbitrary"),
    )(x, x, g.reshape(1, k), w, b3, conv_w, conv_b.reshape(1, n_out), w_gate, b_gate.reshape(1, LANES))


def _mm_res_body(a_ref, w_ref, x_ref, o_ref):
    o_ref[...] = x_ref[...] + jnp.dot(a_ref[...], w_ref[...], preferred_element_type=F32)


def _matmul_residual(a, w, x):
    m, k = a.shape
    n = w.shape[2]
    tm, tn = TILES["out"]
    return pl.pallas_call(
        _mm_res_body,
        name="matmul_residual",
        grid=(m // tm, n // tn),
        in_specs=[pl.BlockSpec((tm, k), lambda i, j: (i, 0)),
                  pl.BlockSpec((None, k, tn), lambda i, j: (0, 0, j)),
                  pl.BlockSpec((tm, tn), lambda i, j: (i, j))],
        out_specs=pl.BlockSpec((tm, tn), lambda i, j: (i, j)),
        out_shape=jax.ShapeDtypeStruct((m, n), F32),
        compiler_params=_params("parallel", "arbitrary"),
    )(a, w, x)


def _ffn_body(*refs, n_cast, final_norm):
    x_hbm, g_ref, w1_ref, w2_ref, fg_ref = refs[:5]
    src_refs = refs[5:5 + n_cast]
    o_ref = refs[5 + n_cast]
    dst_refs = refs[6 + n_cast:6 + 2 * n_cast]
    h_ref, x_buf, x_sem = refs[6 + 2 * n_cast:]
    i, j = pl.program_id(0), pl.program_id(1)
    tm = o_ref.shape[0]

    def x_copy(tile):
        return pltpu.make_async_copy(x_hbm.at[pl.ds(tile * tm, tm), :], x_buf, x_sem)

    @pl.when(j == 0)
    def _():
        @pl.when(i == 0)
        def _():
            x_copy(0).start()

        x_copy(i).wait()
        _rmsnorm_rows(x_buf, g_ref, h_ref)
        o_ref[...] = x_buf[...]

        @pl.when(i + 1 < pl.num_programs(0))
        def _():
            x_copy(i + 1).start()

    for r in range(tm // SUB_ROWS):
        rows = slice(r * SUB_ROWS, (r + 1) * SUB_ROWS)
        a = jnp.dot(h_ref[rows, :], w1_ref[...], preferred_element_type=F32)
        a = jnp.square(jnp.maximum(a, 0.0)).astype(BF16)
        o_ref[rows, :] += jnp.dot(a, w2_ref[...], preferred_element_type=F32)

    for src, dst in zip(src_refs, dst_refs):
        dst[...] = src[...].astype(BF16)

    if final_norm:
        @pl.when(j == pl.num_programs(1) - 1)
        def _():
            _rmsnorm_rows(o_ref, fg_ref, o_ref)


def _ffn(x, g, w1, w2, final_g, final_norm, cast_jobs=()):
    m, d = x.shape
    f = w1.shape[2]
    tm, tf = TILES["ffn"]
    nj = f // tf
    steps = (m // tm) * nj
    src_specs, dst_specs, dst_shapes = [], [], []
    for stack, l in cast_jobs:
        _, r, c = stack.shape
        rb = max(BF16_ROWS, r // steps)
        hold = steps // (r // rb)
        assert r % rb == 0 and steps % (r // rb) == 0
        src_specs.append(pl.BlockSpec((None, rb, c), functools.partial(
            lambda i, j, l, hold: (l, (i * nj + j) // hold, 0), l=l, hold=hold)))
        dst_specs.append(pl.BlockSpec((None, rb, c), functools.partial(
            lambda i, j, hold: (0, (i * nj + j) // hold, 0), hold=hold)))
        dst_shapes.append(jax.ShapeDtypeStruct((1, r, c), BF16))
    outs = pl.pallas_call(
        functools.partial(_ffn_body, n_cast=len(cast_jobs), final_norm=final_norm),
        name="ffn_final" if final_norm else "ffn",
        grid=(m // tm, nj),
        in_specs=[pl.BlockSpec(memory_space=pl.ANY),
                  pl.BlockSpec((1, d), lambda i, j: (0, 0)),
                  pl.BlockSpec((None, d, tf), lambda i, j: (0, 0, j)),
                  pl.BlockSpec((None, tf, d), lambda i, j: (0, j, 0)),
                  pl.BlockSpec((1, d), lambda i, j: (0, 0))] + src_specs,
        out_specs=[pl.BlockSpec((tm, d), lambda i, j: (i, 0))] + dst_specs,
        out_shape=[jax.ShapeDtypeStruct((m, d), F32)] + dst_shapes,
        scratch_shapes=[pltpu.VMEM((tm, d), BF16), pltpu.VMEM((tm, d), F32), pltpu.SemaphoreType.DMA(())],
        compiler_params=_params("arbitrary", "arbitrary"),
    )(x, g.reshape(1, d), w1, w2, final_g.reshape(1, d), *[stack for stack, _ in cast_jobs])
    return outs[0], list(outs[1:])


def _conv_body(gc_ref, gp_ref, x_ref, wdw_ref, bdw_ref, lng_ref, lnb_ref, wout_ref, bout_ref,
               o_ref, ext_ref, y_ref, s_ref, *, seq_tiles, row_chunk):
    ts, d = gc_ref.shape
    kw = wdw_ref.shape[0]
    first = (pl.program_id(0) % seq_tiles) == 0
    lead = CONV_HALO - (kw - 1)

    for p in range(d // LANES):
        lanes = slice(p * LANES, (p + 1) * LANES)
        ext_ref[p, 0:CONV_HALO, :] = jnp.where(first, 0.0, gp_ref[:, lanes])
        ext_ref[p, CONV_HALO:, :] = gc_ref[:, lanes]
        w_rows = [wdw_ref[k:k + 1, lanes] for k in range(kw)]
        bias = jnp.broadcast_to(bdw_ref[:, lanes], (row_chunk, LANES))

        def row_body(rc, carry, p=p, lanes=lanes, w_rows=w_rows, bias=bias):
            base = pl.multiple_of(rc * row_chunk, row_chunk)
            acc = bias
            for k in range(kw):
                acc = acc + w_rows[k] * ext_ref[p, pl.ds(base + lead + k, row_chunk), :]
            y_ref[pl.ds(base, row_chunk), lanes] = acc
            return carry

        lax.fori_loop(0, ts // row_chunk, row_body, 0)

    lng = lng_ref[...]
    lnb = lnb_ref[...]
    rows = 256

    def ln_body(i, carry):
        r = pl.ds(pl.multiple_of(i * rows, rows), rows)
        y = y_ref[r, :]
        mu = jnp.mean(y, axis=-1, keepdims=True)
        yc = y - mu
        var = jnp.mean(yc * yc, axis=-1, keepdims=True)
        yn = yc * lax.rsqrt(var + EPS) * lng + lnb
        s_ref[r, :] = (yn * jax.nn.sigmoid(yn)).astype(BF16)
        return carry

    lax.fori_loop(0, ts // rows, ln_body, 0)
    o_ref[...] = (x_ref[...] + jnp.dot(s_ref[...], wout_ref[...], preferred_element_type=F32)
                  + bout_ref[...])


def _conv_tail(g, x, w_dw, b_dw, ln_g, ln_b, w_out, b_out, seq):
    m, d = g.shape
    kw = w_dw.shape[0]
    ts = TILES["conv_rows"]
    assert kw - 1 <= CONV_HALO and seq % ts == 0 and ts % CONV_HALO == 0
    halo_blocks = ts // CONV_HALO
    row = lambda v: v.reshape(1, d)
    const = lambda i: (0, 0)
    return pl.pallas_call(
        functools.partial(_conv_body, seq_tiles=seq // ts, row_chunk=64),
        name="conv_tail",
        grid=(m // ts,),
        in_specs=[pl.BlockSpec((ts, d), lambda i: (i, 0)),
                  pl.BlockSpec((CONV_HALO, d), lambda i: (jnp.maximum(i * halo_blocks - 1, 0), 0)),
                  pl.BlockSpec((ts, d), lambda i: (i, 0)),
                  pl.BlockSpec((kw, d), const),
                  pl.BlockSpec((1, d), const), pl.BlockSpec((1, d), const), pl.BlockSpec((1, d), const),
                  pl.BlockSpec((None, d, d), lambda i: (0, 0, 0)),
                  pl.BlockSpec((1, d), const)],
        out_specs=pl.BlockSpec((ts, d), lambda i: (i, 0)),
        out_shape=jax.ShapeDtypeStruct((m, d), F32),
        scratch_shapes=[pltpu.VMEM((d // LANES, ts + CONV_HALO, LANES), F32),
                        pltpu.VMEM((ts, d), F32),
                        pltpu.VMEM((ts, d), BF16)],
        compiler_params=_params("arbitrary"),
    )(g, g, x, w_dw, row(b_dw), row(ln_g), row(ln_b), w_out, row(b_out))


def _mlstm_body(q_ref, k_ref, v_ref, so_ref, gate_ref, ng_ref, y_ref, c_ref, n_ref, m_ref):
    L = q_ref.shape[0]
    n_heads = gate_ref.shape[0]
    dqk = q_ref.shape[1] // n_heads
    dv = v_ref.shape[1] // n_heads

    @pl.when(pl.program_id(2) == 0)
    def _():
        c_ref[...] = jnp.zeros(c_ref.shape, F32)
        n_ref[...] = jnp.zeros(n_ref.shape, F32)
        m_ref[...] = jnp.zeros(m_ref.shape, F32)

    t_idx = lax.broadcasted_iota(jnp.int32, (L, L), 0)
    s_idx = lax.broadcasted_iota(jnp.int32, (L, L), 1)
    causal = s_idx <= t_idx
    diag = s_idx == t_idx

    for hd in range(n_heads):
        qk_cols = slice(hd * dqk, (hd + 1) * dqk)
        v_cols = slice(hd * dv, (hd + 1) * dv)
        qb = q_ref[:, qk_cols]
        kb = k_ref[:, qk_cols]
        vb = v_ref[:, v_cols]

        ig_r = gate_ref[hd, 0:1, :]
        lf_r = jax.nn.log_sigmoid(gate_ref[hd, 1:2, :])
        lf_c = jnp.sum(jnp.where(diag, lf_r, 0.0), axis=1, keepdims=True)
        ig_c = jnp.sum(jnp.where(diag, ig_r, 0.0), axis=1, keepdims=True)
        b_c = jnp.sum(jnp.where(causal, lf_r, 0.0), axis=1, keepdims=True)
        b_r = jnp.sum(jnp.where(t_idx <= s_idx, lf_c, 0.0), axis=0, keepdims=True)
        a_r = ig_r - b_r
        a_c = ig_c - b_c
        a_mat = jnp.where(causal, a_r, -jnp.inf)
        m_prev = m_ref[hd, 0:1, 0:1]
        g_c = jnp.maximum(m_prev, jnp.max(a_mat, axis=1, keepdims=True))
        d_mat = jnp.exp(a_mat - g_c)
        inter_c = jnp.exp(m_prev - g_c)
        m_c = b_c + g_c
        b_last = b_c[L - 1:L, :]
        m_new = m_c[L - 1:L, :]

        s = lax.dot_general(qb, kb, (((1,), (1,)), ((), ())), preferred_element_type=F32) * d_mat
        num = (jnp.dot(s.astype(BF16), vb, preferred_element_type=F32)
               + inter_c * jnp.dot(qb, c_ref[hd].astype(BF16), preferred_element_type=F32))
        den = (jnp.sum(s, axis=1, keepdims=True)
               + inter_c * jnp.sum(qb.astype(F32) * n_ref[hd], axis=1, keepdims=True))
        h = num / jnp.maximum(jnp.abs(den), jnp.exp(-m_c))

        w_c = jnp.exp(b_last + a_c - m_new)
        decay = jnp.exp(b_last + m_prev - m_new)
        c_ref[hd] = decay * c_ref[hd] + lax.dot_general(
            kb, (w_c * vb.astype(F32)).astype(BF16), (((0,), (0,)), ((), ())), preferred_element_type=F32)
        n_ref[hd] = decay * n_ref[hd] + jnp.sum(w_c * kb.astype(F32), axis=0, keepdims=True)
        m_ref[hd] = jnp.broadcast_to(m_new, m_ref.shape[1:])

        hn = h * lax.rsqrt(jnp.mean(h * h, axis=1, keepdims=True) + EPS) * ng_ref[:, v_cols]
        y_ref[:, v_cols] = (so_ref[:, v_cols] * hn).astype(y_ref.dtype)


def _mlstm_recurrence(qk, v, sig_o, gates, norm_g, batch, seq):
    heads = MLSTM_HEADS
    hp = MLSTM_HEADS_PER_STEP
    L = MLSTM_CHUNK
    hqk = qk.shape[1] // 2
    hv = v.shape[1]
    dqk, dv = hqk // heads, hv // heads
    groups = heads // hp
    as3 = lambda t: t.reshape(batch, seq, t.shape[1])
    head_blk = lambda width: pl.BlockSpec((None, L, hp * width), lambda b, h, c: (b, c, h))
    return pl.pallas_call(
        _mlstm_body,
        name="mlstm_recurrence",
        grid=(batch, groups, seq // L),
        in_specs=[head_blk(dqk),
                  pl.BlockSpec((None, L, hp * dqk), lambda b, h, c: (b, c, groups + h)),
                  head_blk(dv),
                  head_blk(dv),
                  pl.BlockSpec((None, hp, None, 2, L), lambda b, h, c: (b, h, c, 0, 0)),
                  pl.BlockSpec((1, hp * dv), lambda b, h, c: (0, h))],
        out_specs=head_blk(dv),
        out_shape=jax.ShapeDtypeStruct((batch, seq, hv), BF16),
        scratch_shapes=[pltpu.VMEM((hp, dqk, dv), F32),
                        pltpu.VMEM((hp, 1, dqk), F32),
                        pltpu.VMEM((hp, 8, LANES), F32)],
        compiler_params=_params("parallel", "parallel", "arbitrary"),
    )(as3(qk), as3(qk), as3(v), as3(sig_o), gates, norm_g.reshape(1, hv)).reshape(batch * seq, hv)


def _sgu_body(u_ref, v_ref, x_ref, ng_ref, ws_ref, bst_ref, wout_ref, bout_ref, o_ref,
              vn_ref, gated_ref):
    tm, width = u_ref.shape
    groups, chunk, _ = ws_ref.shape
    gw = width // groups
    _rmsnorm_rows(v_ref, ng_ref, vn_ref)
    t_idx = lax.broadcasted_iota(jnp.int32, (chunk, chunk), 0)
    s_idx = lax.broadcasted_iota(jnp.int32, (chunk, chunk), 1)
    tril = (s_idx <= t_idx).astype(F32)
    for g in range(groups):
        ws_g = (ws_ref[g] * tril).astype(BF16)
        bias_c = bst_ref[:, g:g + 1]
        cols = slice(g * gw, (g + 1) * gw)
        for c in range(tm // chunk):
            rows = slice(c * chunk, (c + 1) * chunk)
            mixed = jnp.dot(ws_g, vn_ref[rows, cols], preferred_element_type=F32) + bias_c
            gated_ref[rows, cols] = (u_ref[rows, cols] * mixed).astype(BF16)
    o_ref[...] = (x_ref[...] + jnp.dot(gated_ref[...], wout_ref[...], preferred_element_type=F32)
                  + bout_ref[...])


def _sgu_tail(p, x, norm_g, w_s, b_s, w_out, b_out):
    m, d = x.shape
    width = norm_g.shape[0]
    groups, chunk, _ = w_s.shape
    tm = TILES["sgu_rows"]
    const2 = lambda i: (0, 0)
    return pl.pallas_call(
        _sgu_body,
        name="sgu_tail",
        grid=(m // tm,),
        in_specs=[pl.BlockSpec((tm, width), lambda i: (i, 0)),
                  pl.BlockSpec((tm, width), lambda i: (i, 1)),
                  pl.BlockSpec((tm, d), lambda i: (i, 0)),
                  pl.BlockSpec((1, width), const2),
                  pl.BlockSpec((groups, chunk, chunk), lambda i: (0, 0, 0)),
                  pl.BlockSpec((chunk, groups), const2),
                  pl.BlockSpec((None, width, d), lambda i: (0, 0, 0)),
                  pl.BlockSpec((1, d), const2)],
        out_specs=pl.BlockSpec((tm, d), lambda i: (i, 0)),
        out_shape=jax.ShapeDtypeStruct((m, d), F32),
        scratch_shapes=[pltpu.VMEM((tm, width), BF16), pltpu.VMEM((tm, width), BF16)],
        compiler_params=_params("parallel"),
    )(p, p, x, norm_g.reshape(1, width), w_s, jnp.transpose(b_s), w_out, b_out.reshape(1, d))


def _glu(val, gate):
    return val * jax.nn.sigmoid(gate)


def _identity(a):
    return a


def _sigmoid(a):
    return jax.nn.sigmoid(a)


def _gelu(a):
    return 0.5 * a * (1.0 + lax.erf(a * (2.0 ** -0.5)))


def _mlstm_layer(x, ng, w_in, b_in, w_qkconv, b_qkconv, norm_g, w_out, batch, seq):
    heads = MLSTM_HEADS
    n_qk = w_qkconv.shape[1]
    hv = norm_g.shape[0]
    dqk = n_qk // 2 // heads
    n_main = n_qk + 2 * hv
    w_gate = jnp.pad(w_in[0, :, n_main:], ((0, 0), (0, LANES - 2 * heads)))
    b_gate = jnp.pad(b_in[n_main:], (0, LANES - 2 * heads))
    qk, gates = _qk_projection(x, ng, w_in, b_in, w_qkconv, b_qkconv, w_gate, b_gate, n_qk // 2,
                               dqk ** -0.5, seq)
    v = _norm_matmul(x, ng, w_in, b_in, (n_qk,), hv, _identity, out_dtype=BF16)
    sig_o = _norm_matmul(x, ng, w_in, b_in, (n_qk + hv,), hv, _sigmoid)
    nc = seq // MLSTM_CHUNK
    gates = gates[:, :2 * heads].reshape(batch, nc, MLSTM_CHUNK, 2, heads)
    gates = jnp.transpose(gates, (0, 4, 1, 3, 2))
    y = _mlstm_recurrence(qk, v, sig_o, gates, norm_g, batch, seq)
    return _matmul_residual(y, w_out, x)


def kernel(x, norm_mix_g, norm_ffn_g, final_g, conv_w_in, conv_b_in, conv_w_dw, conv_b_dw, conv_ln_g, conv_ln_b, conv_w_out, conv_b_out, mlstm_w_in, mlstm_b_in, mlstm_w_qkconv, mlstm_b_qkconv, mlstm_norm_g, mlstm_w_out, sgu_w_in, sgu_b_in, sgu_norm_g, sgu_w_s, sgu_b_s, sgu_w_out, sgu_b_out, ffn_w1, ffn_w2):
    batch, seq, d = x.shape
    depth = norm_mix_g.shape[0]
    mixer_weights = ((conv_w_in, conv_w_out), (mlstm_w_in, mlstm_w_out), (sgu_w_in, sgu_w_out))

    def layer_weight_jobs(i):
        return [(w, i // N_MIXERS) for w in mixer_weights[i % N_MIXERS]] + [(ffn_w1, i), (ffn_w2, i)]

    w_mix_in, w_mix_out, w1, w2 = (w[l:l + 1].astype(BF16) for w, l in layer_weight_jobs(0))
    h = x.reshape(batch * seq, d)
    for i in range(depth):
        kind, j = i % N_MIXERS, i // N_MIXERS
        ng = norm_mix_g[i]
        if kind == 0:
            g = _norm_matmul(h, ng, w_mix_in, conv_b_in[j], (0, d), d, _glu)
            h = _conv_tail(g, h, conv_w_dw[j], conv_b_dw[j], conv_ln_g[j], conv_ln_b[j], w_mix_out,
                           conv_b_out[j], seq)
        elif kind == 1:
            h = _mlstm_layer(h, ng, w_mix_in, mlstm_b_in[j], mlstm_w_qkconv[j], mlstm_b_qkconv[j],
                             mlstm_norm_g[j], w_mix_out, batch, seq)
        else:
            p = _norm_matmul(h, ng, w_mix_in, sgu_b_in[j], (0,), sgu_w_in.shape[2], _gelu)
            h = _sgu_tail(p, h, sgu_norm_g[j], sgu_w_s[j], sgu_b_s[j], w_mix_out, sgu_b_out[j])
        last = i == depth - 1
        jobs = [] if last else layer_weight_jobs(i + 1)
        inside = [n for n, (w, _) in enumerate(jobs) if w.shape[2] % LANES == 0]
        h, cast = _ffn(h, norm_ffn_g[i], w1, w2, final_g, final_norm=last,
                       cast_jobs=[jobs[n] for n in inside])
        if not last:
            nxt = {n: c for n, c in zip(inside, cast)}
            w_mix_in, w_mix_out, w1, w2 = (nxt[n] if n in nxt else w[l:l + 1].astype(BF16)
                                           for n, (w, l) in enumerate(jobs))
    return h.reshape(batch, seq, d)
```

```python
import functools

import jax
import jax.numpy as jnp
from jax import lax
from jax.experimental import pallas as pl
from jax.experimental.pallas import tpu as pltpu

EPS = 1e-6
N_MIXERS = 3
MLSTM_HEADS = 8
MLSTM_CHUNK = 256
MLSTM_HEADS_PER_STEP = 2
SGU_CHUNK = 128
SGU_GROUPS = 8
CONV_HALO = 32
QK_HALO = 16
LANES = 128
BF16_ROWS = 16
SUB_ROWS = 256

F32 = jnp.float32
BF16 = jnp.bfloat16

V7X_VMEM_BYTES = 64 * 1024 * 1024
VMEM_LIMIT = V7X_VMEM_BYTES - 8 * 1024 * 1024

TILES = dict(
    ffn=(1024, 1024),
    proj=(1024, 1024),
    proj_normed=(2048, 1024),
    out=(1024, 512),
    conv_rows=512,
    sgu_rows=512,
)


def _params(*sem):
    return pltpu.CompilerParams(dimension_semantics=sem, vmem_limit_bytes=VMEM_LIMIT)


def _rmsnorm_rows(x_ref, g_ref, h_ref, dst_offset=0, rows=256):
    g = g_ref[...]
    rows = min(rows, x_ref.shape[0])

    def body(i, carry):
        start = pl.multiple_of(i * rows, rows)
        x = x_ref[pl.ds(start, rows), :]
        ms = jnp.mean(x * x, axis=-1, keepdims=True)
        h_ref[pl.ds(start + dst_offset, rows), :] = (x * lax.rsqrt(ms + EPS) * g).astype(h_ref.dtype)
        return carry

    lax.fori_loop(0, x_ref.shape[0] // rows, body, 0)


def _norm_mm_body(*refs, n_w, epilogue, normed):
    if normed:
        h_ref, refs = refs[0], refs[1:]
    else:
        x_ref, g_ref, h_ref, refs = refs[0], refs[1], refs[-1], refs[2:-1]

        @pl.when(pl.program_id(1) == 0)
        def _():
            _rmsnorm_rows(x_ref, g_ref, h_ref)

    w_refs, b_refs, o_ref = refs[:n_w], refs[n_w:2 * n_w], refs[2 * n_w]

    for r in range(o_ref.shape[0] // SUB_ROWS):
        rows = slice(r * SUB_ROWS, (r + 1) * SUB_ROWS)
        h = h_ref[rows, :]
        accs = [jnp.dot(h, w[...], preferred_element_type=F32) + b[...]
                for w, b in zip(w_refs, b_refs)]
        o_ref[rows, :] = epilogue(*accs).astype(o_ref.dtype)


def _norm_matmul(x, g, w, b, col_offsets, n_out, epilogue, out_dtype=F32):
    m, k = x.shape
    normed = x.dtype == BF16
    tm, tn = TILES["proj_normed" if normed else "proj"]
    n_w = len(col_offsets)
    col_map = lambda i, j, o: (0, 0, j + o)
    w_specs = [pl.BlockSpec((None, k, tn), functools.partial(col_map, o=off // tn)) for off in col_offsets]
    b_specs = [pl.BlockSpec((None, 1, tn), functools.partial(col_map, o=off // tn)) for off in col_offsets]
    b3 = b.reshape(1, 1, -1)
    return pl.pallas_call(
        functools.partial(_norm_mm_body, n_w=n_w, epilogue=epilogue, normed=normed),
        name="norm_matmul_" + epilogue.__name__.strip("_"),
        grid=(m // tm, n_out // tn),
        in_specs=[pl.BlockSpec((tm, k), lambda i, j: (i, 0))]
        + ([] if normed else [pl.BlockSpec((1, k), lambda i, j: (0, 0))]) + w_specs + b_specs,
        out_specs=pl.BlockSpec((tm, tn), lambda i, j: (i, j)),
        out_shape=jax.ShapeDtypeStruct((m, n_out), out_dtype),
        scratch_shapes=[] if normed else [pltpu.VMEM((tm, k), BF16)],
        compiler_params=_params("parallel", "arbitrary"),
    )(x, *([] if normed else [g.reshape(1, k)]), *([w] * n_w), *([b3] * n_w))


def _qk_proj_body(h_ref, hh_ref, w_ref, b_ref, cw_ref, cb_ref, wg_ref, bg_ref, o_ref, og_ref, p_ref,
                  *, seq_tiles, q_tiles, q_scale):
    i, j = pl.program_id(0), pl.program_id(1)
    tm, tn = o_ref.shape
    kw = cw_ref.shape[0]

    @pl.when(j == 0)
    def _():
        og_ref[...] = jnp.dot(h_ref[...], wg_ref[...], preferred_element_type=F32) + bg_ref[...]

    first = (i % seq_tiles) == 0
    scale = jnp.where(j < q_tiles, q_scale, 1.0)
    for r in range(tm // SUB_ROWS):
        lo = QK_HALO + r * SUB_ROWS
        lhs = h_ref[r * SUB_ROWS:(r + 1) * SUB_ROWS, :]
        if r == 0:
            lhs = jnp.concatenate([hh_ref[...], lhs], axis=0)
        acc = jnp.dot(lhs, w_ref[...], preferred_element_type=F32) + b_ref[...]
        for p in range(tn // LANES):
            lanes = slice(p * LANES, (p + 1) * LANES)
            if r == 0:
                p_ref[p, 0:QK_HALO, :] = jnp.where(first, 0.0, acc[0:QK_HALO, lanes])
            p_ref[p, lo:lo + SUB_ROWS, :] = acc[acc.shape[0] - SUB_ROWS:, lanes]
            y = cb_ref[:, lanes]
            for t in range(kw):
                start = lo - (kw - 1) + t
                y = y + cw_ref[t:t + 1, lanes] * p_ref[p, start:start + SUB_ROWS, :]
            o_ref[r * SUB_ROWS:(r + 1) * SUB_ROWS, lanes] = (y * jax.nn.sigmoid(y) * scale).astype(o_ref.dtype)


def _qk_projection(h, w, b, conv_w, conv_b, w_gate, b_gate, n_q, q_scale, seq):
    m, k = h.shape
    tm, tn = TILES["proj_normed"]
    n_out = conv_w.shape[1]
    kw = conv_w.shape[0]
    assert kw - 1 <= QK_HALO and seq % tm == 0 and tm % QK_HALO == 0
    halo_blocks = tm // QK_HALO
    b3 = b.reshape(1, 1, -1)
    return pl.pallas_call(
        functools.partial(_qk_proj_body, seq_tiles=seq // tm, q_tiles=n_q // tn, q_scale=q_scale),
        name="mlstm_qk_projection",
        grid=(m // tm, n_out // tn),
        in_specs=[pl.BlockSpec((tm, k), lambda i, j: (i, 0)),
                  pl.BlockSpec((QK_HALO, k), lambda i, j: (jnp.maximum(i * halo_blocks - 1, 0), 0)),
                  pl.BlockSpec((None, k, tn), lambda i, j: (0, 0, j)),
                  pl.BlockSpec((None, 1, tn), lambda i, j: (0, 0, j)),
                  pl.BlockSpec((kw, tn), lambda i, j: (0, j)),
                  pl.BlockSpec((1, tn), lambda i, j: (0, j)),
                  pl.BlockSpec((k, LANES), lambda i, j: (0, 0)),
                  pl.BlockSpec((1, LANES), lambda i, j: (0, 0))],
        out_specs=[pl.BlockSpec((tm, tn), lambda i, j: (i, j)),
                   pl.BlockSpec((tm, LANES), lambda i, j: (i, 0))],
        out_shape=[jax.ShapeDtypeStruct((m, n_out), BF16), jax.ShapeDtypeStruct((m, LANES), F32)],
        scratch_shapes=[pltpu.VMEM((tn // LANES, QK_HALO + tm, LANES), F32)],
        compiler_params=_params("parallel", "arbitrary"),
    )(h, h, w, b3, conv_w, conv_b.reshape(1, n_out), w_gate, b_gate.reshape(1, LANES))


def _mm_res_body(a_ref, w_ref, x_ref, o_ref):
    o_ref[...] = x_ref[...] + jnp.dot(a_ref[...], w_ref[...], preferred_element_type=F32)


def _matmul_residual(a, w, x):
    m, k = a.shape
    n = w.shape[2]
    tm, tn = TILES["out"]
    return pl.pallas_call(
        _mm_res_body,
        name="matmul_residual",
        grid=(m // tm, n // tn),
        in_specs=[pl.BlockSpec((tm, k), lambda i, j: (i, 0)),
                  pl.BlockSpec((None, k, tn), lambda i, j: (0, 0, j)),
                  pl.BlockSpec((tm, tn), lambda i, j: (i, j))],
        out_specs=pl.BlockSpec((tm, tn), lambda i, j: (i, j)),
        out_shape=jax.ShapeDtypeStruct((m, n), F32),
        compiler_params=_params("parallel", "arbitrary"),
    )(a, w, x)


def _ffn_body(*refs, n_cast, final_norm):
    x_hbm, g_ref, w1_ref, w2_ref, fg_ref = refs[:5]
    src_refs = refs[5:5 + n_cast]
    h_ref, x_buf, x_sem = refs[-3:]
    outs = refs[5 + n_cast:-3]
    o_ref, normed_ref = (outs[0], outs[0]) if final_norm else outs[:2]
    dst_refs = outs[len(outs) - n_cast:]
    i, j = pl.program_id(0), pl.program_id(1)
    tm = o_ref.shape[0]

    def x_copy(tile):
        return pltpu.make_async_copy(x_hbm.at[pl.ds(tile * tm, tm), :], x_buf, x_sem)

    @pl.when(j == 0)
    def _():
        @pl.when(i == 0)
        def _():
            x_copy(0).start()

        x_copy(i).wait()
        _rmsnorm_rows(x_buf, g_ref, h_ref)
        o_ref[...] = x_buf[...]

        @pl.when(i + 1 < pl.num_programs(0))
        def _():
            x_copy(i + 1).start()

    for r in range(tm // SUB_ROWS):
        rows = slice(r * SUB_ROWS, (r + 1) * SUB_ROWS)
        a = jnp.dot(h_ref[rows, :], w1_ref[...], preferred_element_type=F32)
        a = jnp.square(jnp.maximum(a, 0.0)).astype(BF16)
        o_ref[rows, :] += jnp.dot(a, w2_ref[...], preferred_element_type=F32)

    for src, dst in zip(src_refs, dst_refs):
        dst[...] = src[...].astype(BF16)

    @pl.when(j == pl.num_programs(1) - 1)
    def _():
        _rmsnorm_rows(o_ref, fg_ref, normed_ref)


def _ffn(x, g, w1, w2, next_g, final_norm, cast_jobs=()):
    m, d = x.shape
    f = w1.shape[2]
    tm, tf = TILES["ffn"]
    nj = f // tf
    steps = (m // tm) * nj
    src_specs, dst_specs, dst_shapes = [], [], []
    for stack, l in cast_jobs:
        _, r, c = stack.shape
        rb = max(BF16_ROWS, r // steps)
        hold = steps // (r // rb)
        assert r % rb == 0 and steps % (r // rb) == 0
        src_specs.append(pl.BlockSpec((None, rb, c), functools.partial(
            lambda i, j, l, hold: (l, (i * nj + j) // hold, 0), l=l, hold=hold)))
        dst_specs.append(pl.BlockSpec((None, rb, c), functools.partial(
            lambda i, j, hold: (0, (i * nj + j) // hold, 0), hold=hold)))
        dst_shapes.append(jax.ShapeDtypeStruct((1, r, c), BF16))
    normed_spec = [] if final_norm else [
        pl.BlockSpec((tm, d), lambda i, j: (i, 0), pipeline_mode=pl.Buffered(1))]
    normed_shape = [] if final_norm else [jax.ShapeDtypeStruct((m, d), BF16)]
    outs = pl.pallas_call(
        functools.partial(_ffn_body, n_cast=len(cast_jobs), final_norm=final_norm),
        name="ffn_final" if final_norm else "ffn",
        grid=(m // tm, nj),
        in_specs=[pl.BlockSpec(memory_space=pl.ANY),
                  pl.BlockSpec((1, d), lambda i, j: (0, 0)),
                  pl.BlockSpec((None, d, tf), lambda i, j: (0, 0, j)),
                  pl.BlockSpec((None, tf, d), lambda i, j: (0, j, 0)),
                  pl.BlockSpec((1, d), lambda i, j: (0, 0))] + src_specs,
        out_specs=[pl.BlockSpec((tm, d), lambda i, j: (i, 0))] + normed_spec + dst_specs,
        out_shape=[jax.ShapeDtypeStruct((m, d), F32)] + normed_shape + dst_shapes,
        scratch_shapes=[pltpu.VMEM((tm, d), BF16), pltpu.VMEM((tm, d), F32), pltpu.SemaphoreType.DMA(())],
        compiler_params=_params("arbitrary", "arbitrary"),
    )(x, g.reshape(1, d), w1, w2, next_g.reshape(1, d), *[stack for stack, _ in cast_jobs])
    if final_norm:
        return outs[0], None, list(outs[1:])
    return outs[0], outs[1], list(outs[2:])


def _conv_body(gc_ref, gp_ref, x_ref, wdw_ref, bdw_ref, lng_ref, lnb_ref, wout_ref, bout_ref,
               o_ref, ext_ref, y_ref, s_ref, *, seq_tiles, row_chunk):
    ts, d = gc_ref.shape
    kw = wdw_ref.shape[0]
    first = (pl.program_id(0) % seq_tiles) == 0
    lead = CONV_HALO - (kw - 1)

    for p in range(d // LANES):
        lanes = slice(p * LANES, (p + 1) * LANES)
        ext_ref[p, 0:CONV_HALO, :] = jnp.where(first, 0.0, gp_ref[:, lanes])
        ext_ref[p, CONV_HALO:, :] = gc_ref[:, lanes]
        w_rows = [wdw_ref[k:k + 1, lanes] for k in range(kw)]
        bias = jnp.broadcast_to(bdw_ref[:, lanes], (row_chunk, LANES))

        def row_body(rc, carry, p=p, lanes=lanes, w_rows=w_rows, bias=bias):
            base = pl.multiple_of(rc * row_chunk, row_chunk)
            acc = bias
            for k in range(kw):
                acc = acc + w_rows[k] * ext_ref[p, pl.ds(base + lead + k, row_chunk), :]
            y_ref[pl.ds(base, row_chunk), lanes] = acc
            return carry

        lax.fori_loop(0, ts // row_chunk, row_body, 0)

    lng = lng_ref[...]
    lnb = lnb_ref[...]
    rows = 256

    def ln_body(i, carry):
        r = pl.ds(pl.multiple_of(i * rows, rows), rows)
        y = y_ref[r, :]
        mu = jnp.mean(y, axis=-1, keepdims=True)
        yc = y - mu
        var = jnp.mean(yc * yc, axis=-1, keepdims=True)
        yn = yc * lax.rsqrt(var + EPS) * lng + lnb
        s_ref[r, :] = (yn * jax.nn.sigmoid(yn)).astype(BF16)
        return carry

    lax.fori_loop(0, ts // rows, ln_body, 0)
    o_ref[...] = (x_ref[...] + jnp.dot(s_ref[...], wout_ref[...], preferred_element_type=F32)
                  + bout_ref[...])


def _conv_tail(g, x, w_dw, b_dw, ln_g, ln_b, w_out, b_out, seq):
    m, d = g.shape
    kw = w_dw.shape[0]
    ts = TILES["conv_rows"]
    assert kw - 1 <= CONV_HALO and seq % ts == 0 and ts % CONV_HALO == 0
    halo_blocks = ts // CONV_HALO
    row = lambda v: v.reshape(1, d)
    const = lambda i: (0, 0)
    return pl.pallas_call(
        functools.partial(_conv_body, seq_tiles=seq // ts, row_chunk=64),
        name="conv_tail",
        grid=(m // ts,),
        in_specs=[pl.BlockSpec((ts, d), lambda i: (i, 0)),
                  pl.BlockSpec((CONV_HALO, d), lambda i: (jnp.maximum(i * halo_blocks - 1, 0), 0)),
                  pl.BlockSpec((ts, d), lambda i: (i, 0)),
                  pl.BlockSpec((kw, d), const),
                  pl.BlockSpec((1, d), const), pl.BlockSpec((1, d), const), pl.BlockSpec((1, d), const),
                  pl.BlockSpec((None, d, d), lambda i: (0, 0, 0)),
                  pl.BlockSpec((1, d), const)],
        out_specs=pl.BlockSpec((ts, d), lambda i: (i, 0)),
        out_shape=jax.ShapeDtypeStruct((m, d), F32),
        scratch_shapes=[pltpu.VMEM((d // LANES, ts + CONV_HALO, LANES), F32),
                        pltpu.VMEM((ts, d), F32),
                        pltpu.VMEM((ts, d), BF16)],
        compiler_params=_params("arbitrary"),
    )(g, g, x, w_dw, row(b_dw), row(ln_g), row(ln_b), w_out, row(b_out))


def _mlstm_body(q_ref, k_ref, v_ref, so_ref, gate_ref, ng_ref, y_ref, c_ref, n_ref, m_ref):
    L = q_ref.shape[0]
    n_heads = gate_ref.shape[0]
    dqk = q_ref.shape[1] // n_heads
    dv = v_ref.shape[1] // n_heads

    @pl.when(pl.program_id(2) == 0)
    def _():
        c_ref[...] = jnp.zeros(c_ref.shape, F32)
        n_ref[...] = jnp.zeros(n_ref.shape, F32)
        m_ref[...] = jnp.zeros(m_ref.shape, F32)

    t_idx = lax.broadcasted_iota(jnp.int32, (L, L), 0)
    s_idx = lax.broadcasted_iota(jnp.int32, (L, L), 1)
    causal = s_idx <= t_idx
    diag = s_idx == t_idx

    for hd in range(n_heads):
        qk_cols = slice(hd * dqk, (hd + 1) * dqk)
        v_cols = slice(hd * dv, (hd + 1) * dv)
        qb = q_ref[:, qk_cols]
        kb = k_ref[:, qk_cols]
        vb = v_ref[:, v_cols]

        ig_r = gate_ref[hd, 0:1, :]
        lf_r = jax.nn.log_sigmoid(gate_ref[hd, 1:2, :])
        lf_c = jnp.sum(jnp.where(diag, lf_r, 0.0), axis=1, keepdims=True)
        ig_c = jnp.sum(jnp.where(diag, ig_r, 0.0), axis=1, keepdims=True)
        b_c = jnp.sum(jnp.where(causal, lf_r, 0.0), axis=1, keepdims=True)
        b_r = jnp.sum(jnp.where(t_idx <= s_idx, lf_c, 0.0), axis=0, keepdims=True)
        a_r = ig_r - b_r
        a_c = ig_c - b_c
        a_mat = jnp.where(causal, a_r, -jnp.inf)
        m_prev = m_ref[hd, 0:1, 0:1]
        g_c = jnp.maximum(m_prev, jnp.max(a_mat, axis=1, keepdims=True))
        d_mat = jnp.exp(a_mat - g_c)
        inter_c = jnp.exp(m_prev - g_c)
        m_c = b_c + g_c
        b_last = b_c[L - 1:L, :]
        m_new = m_c[L - 1:L, :]

        s = lax.dot_general(qb, kb, (((1,), (1,)), ((), ())), preferred_element_type=F32) * d_mat
        num = (jnp.dot(s.astype(BF16), vb, preferred_element_type=F32)
               + inter_c * jnp.dot(qb, c_ref[hd].astype(BF16), preferred_element_type=F32))
        den = (jnp.sum(s, axis=1, keepdims=True)
               + inter_c * jnp.sum(qb.astype(F32) * n_ref[hd], axis=1, keepdims=True))
        h = num / jnp.maximum(jnp.abs(den), jnp.exp(-m_c))

        w_c = jnp.exp(b_last + a_c - m_new)
        decay = jnp.exp(b_last + m_prev - m_new)
        c_ref[hd] = decay * c_ref[hd] + lax.dot_general(
            kb, (w_c * vb.astype(F32)).astype(BF16), (((0,), (0,)), ((), ())), preferred_element_type=F32)
        n_ref[hd] = decay * n_ref[hd] + jnp.sum(w_c * kb.astype(F32), axis=0, keepdims=True)
        m_ref[hd] = jnp.broadcast_to(m_new, m_ref.shape[1:])

        hn = h * lax.rsqrt(jnp.mean(h * h, axis=1, keepdims=True) + EPS) * ng_ref[:, v_cols]
        y_ref[:, v_cols] = (so_ref[:, v_cols] * hn).astype(y_ref.dtype)


def _mlstm_recurrence(qk, v, sig_o, gates, norm_g, batch, seq):
    heads = MLSTM_HEADS
    hp = MLSTM_HEADS_PER_STEP
    L = MLSTM_CHUNK
    hqk = qk.shape[1] // 2
    hv = v.shape[1]
    dqk, dv = hqk // heads, hv // heads
    groups = heads // hp
    as3 = lambda t: t.reshape(batch, seq, t.shape[1])
    head_blk = lambda width: pl.BlockSpec((None, L, hp * width), lambda b, h, c: (b, c, h))
    return pl.pallas_call(
        _mlstm_body,
        name="mlstm_recurrence",
        grid=(batch, groups, seq // L),
        in_specs=[head_blk(dqk),
                  pl.BlockSpec((None, L, hp * dqk), lambda b, h, c: (b, c, groups + h)),
                  head_blk(dv),
                  head_blk(dv),
                  pl.BlockSpec((None, hp, None, 2, L), lambda b, h, c: (b, h, c, 0, 0)),
                  pl.BlockSpec((1, hp * dv), lambda b, h, c: (0, h))],
        out_specs=head_blk(dv),
        out_shape=jax.ShapeDtypeStruct((batch, seq, hv), BF16),
        scratch_shapes=[pltpu.VMEM((hp, dqk, dv), F32),
                        pltpu.VMEM((hp, 1, dqk), F32),
                        pltpu.VMEM((hp, 8, LANES), F32)],
        compiler_params=_params("parallel", "parallel", "arbitrary"),
    )(as3(qk), as3(qk), as3(v), as3(sig_o), gates, norm_g.reshape(1, hv)).reshape(batch * seq, hv)


def _sgu_body(u_ref, v_ref, x_ref, ng_ref, ws_ref, bst_ref, wout_ref, bout_ref, o_ref,
              vn_ref, gated_ref):
    tm, width = u_ref.shape
    groups, chunk, _ = ws_ref.shape
    gw = width // groups
    _rmsnorm_rows(v_ref, ng_ref, vn_ref)
    t_idx = lax.broadcasted_iota(jnp.int32, (chunk, chunk), 0)
    s_idx = lax.broadcasted_iota(jnp.int32, (chunk, chunk), 1)
    tril = (s_idx <= t_idx).astype(F32)
    for g in range(groups):
        ws_g = (ws_ref[g] * tril).astype(BF16)
        bias_c = bst_ref[:, g:g + 1]
        cols = slice(g * gw, (g + 1) * gw)
        for c in range(tm // chunk):
            rows = slice(c * chunk, (c + 1) * chunk)
            mixed = jnp.dot(ws_g, vn_ref[rows, cols], preferred_element_type=F32) + bias_c
            gated_ref[rows, cols] = (u_ref[rows, cols] * mixed).astype(BF16)
    o_ref[...] = (x_ref[...] + jnp.dot(gated_ref[...], wout_ref[...], preferred_element_type=F32)
                  + bout_ref[...])


def _sgu_tail(p, x, norm_g, w_s, b_s, w_out, b_out):
    m, d = x.shape
    width = norm_g.shape[0]
    groups, chunk, _ = w_s.shape
    tm = TILES["sgu_rows"]
    const2 = lambda i: (0, 0)
    return pl.pallas_call(
        _sgu_body,
        name="sgu_tail",
        grid=(m // tm,),
        in_specs=[pl.BlockSpec((tm, width), lambda i: (i, 0)),
                  pl.BlockSpec((tm, width), lambda i: (i, 1)),
                  pl.BlockSpec((tm, d), lambda i: (i, 0)),
                  pl.BlockSpec((1, width), const2),
                  pl.BlockSpec((groups, chunk, chunk), lambda i: (0, 0, 0)),
                  pl.BlockSpec((chunk, groups), const2),
                  pl.BlockSpec((None, width, d), lambda i: (0, 0, 0)),
                  pl.BlockSpec((1, d), const2)],
        out_specs=pl.BlockSpec((tm, d), lambda i: (i, 0)),
        out_shape=jax.ShapeDtypeStruct((m, d), F32),
        scratch_shapes=[pltpu.VMEM((tm, width), BF16), pltpu.VMEM((tm, width), BF16)],
        compiler_params=_params("parallel"),
    )(p, p, x, norm_g.reshape(1, width), w_s, jnp.transpose(b_s), w_out, b_out.reshape(1, d))


def _glu(val, gate):
    return val * jax.nn.sigmoid(gate)


def _identity(a):
    return a


def _sigmoid(a):
    return jax.nn.sigmoid(a)


def _gelu(a):
    return 0.5 * a * (1.0 + lax.erf(a * (2.0 ** -0.5)))


def _mlstm_layer(x, normed, w_in, b_in, w_qkconv, b_qkconv, norm_g, w_out, batch, seq):
    heads = MLSTM_HEADS
    n_qk = w_qkconv.shape[1]
    hv = norm_g.shape[0]
    dqk = n_qk // 2 // heads
    n_main = n_qk + 2 * hv
    w_gate = jnp.pad(w_in[0, :, n_main:], ((0, 0), (0, LANES - 2 * heads)))
    b_gate = jnp.pad(b_in[n_main:], (0, LANES - 2 * heads))
    qk, gates = _qk_projection(normed, w_in, b_in, w_qkconv, b_qkconv, w_gate, b_gate, n_qk // 2,
                               dqk ** -0.5, seq)
    v = _norm_matmul(normed, None, w_in, b_in, (n_qk,), hv, _identity, out_dtype=BF16)
    sig_o = _norm_matmul(normed, None, w_in, b_in, (n_qk + hv,), hv, _sigmoid)
    nc = seq // MLSTM_CHUNK
    gates = gates[:, :2 * heads].reshape(batch, nc, MLSTM_CHUNK, 2, heads)
    gates = jnp.transpose(gates, (0, 4, 1, 3, 2))
    y = _mlstm_recurrence(qk, v, sig_o, gates, norm_g, batch, seq)
    return _matmul_residual(y, w_out, x)


def kernel(x, norm_mix_g, norm_ffn_g, final_g, conv_w_in, conv_b_in, conv_w_dw, conv_b_dw, conv_ln_g, conv_ln_b, conv_w_out, conv_b_out, mlstm_w_in, mlstm_b_in, mlstm_w_qkconv, mlstm_b_qkconv, mlstm_norm_g, mlstm_w_out, sgu_w_in, sgu_b_in, sgu_norm_g, sgu_w_s, sgu_b_s, sgu_w_out, sgu_b_out, ffn_w1, ffn_w2):
    batch, seq, d = x.shape
    depth = norm_mix_g.shape[0]
    mixer_weights = ((conv_w_in, conv_w_out), (mlstm_w_in, mlstm_w_out), (sgu_w_in, sgu_w_out))

    def layer_weight_jobs(i):
        return [(w, i // N_MIXERS) for w in mixer_weights[i % N_MIXERS]] + [(ffn_w1, i), (ffn_w2, i)]

    w_mix_in, w_mix_out, w1, w2 = (w[l:l + 1].astype(BF16) for w, l in layer_weight_jobs(0))
    h = x.reshape(batch * seq, d)
    normed = None
    for i in range(depth):
        kind, j = i % N_MIXERS, i // N_MIXERS
        src, ng = (h, norm_mix_g[i]) if normed is None else (normed, None)
        if kind == 0:
            g = _norm_matmul(src, ng, w_mix_in, conv_b_in[j], (0, d), d, _glu)
            h = _conv_tail(g, h, conv_w_dw[j], conv_b_dw[j], conv_ln_g[j], conv_ln_b[j], w_mix_out,
                           conv_b_out[j], seq)
        elif kind == 1:
            assert normed is not None, "the mLSTM projections take the normalised stream of a preceding layer"
            h = _mlstm_layer(h, normed, w_mix_in, mlstm_b_in[j], mlstm_w_qkconv[j], mlstm_b_qkconv[j],
                             mlstm_norm_g[j], w_mix_out, batch, seq)
        else:
            p = _norm_matmul(src, ng, w_mix_in, sgu_b_in[j], (0,), sgu_w_in.shape[2], _gelu)
            h = _sgu_tail(p, h, sgu_norm_g[j], sgu_w_s[j], sgu_b_s[j], w_mix_out, sgu_b_out[j])
        last = i == depth - 1
        jobs = [] if last else layer_weight_jobs(i + 1)
        inside = [n for n, (w, _) in enumerate(jobs) if w.shape[2] % LANES == 0]
        h, normed, cast = _ffn(h, norm_ffn_g[i], w1, w2, final_g if last else norm_mix_g[i + 1],
                               final_norm=last, cast_jobs=[jobs[n] for n in inside])
        if not last:
            nxt = {n: c for n, c in zip(inside, cast)}
            w_mix_in, w_mix_out, w1, w2 = (nxt[n] if n in nxt else w[l:l + 1].astype(BF16)
                                           for n, (w, l) in enumerate(jobs))
    return h.reshape(batch, seq, d)
```

```python
import functools

import jax
import jax.numpy as jnp
from jax import lax
from jax.experimental import pallas as pl
from jax.experimental.pallas import tpu as pltpu

EPS = 1e-6
N_MIXERS = 3
MLSTM_HEADS = 8
MLSTM_CHUNK = 256
MLSTM_HEADS_PER_STEP = 2
SGU_CHUNK = 128
SGU_GROUPS = 8
CONV_HALO = 32
QK_HALO = 16
LANES = 128
BF16_ROWS = 16
SUB_ROWS = 256

F32 = jnp.float32
BF16 = jnp.bfloat16

V7X_VMEM_BYTES = 64 * 1024 * 1024
VMEM_LIMIT = V7X_VMEM_BYTES - 4 * 1024 * 1024

TILES = dict(
    ffn=(1024, 1024),
    proj=(1024, 1024),
    proj_normed=(2048, 1024),
    out=(1024, 512),
    conv_rows=512,
    sgu_rows=512,
)


def _params(*sem):
    return pltpu.CompilerParams(dimension_semantics=sem, vmem_limit_bytes=VMEM_LIMIT)


def _rmsnorm_rows(x_ref, g_ref, h_ref, dst_offset=0, rows=256):
    g = g_ref[...]
    rows = min(rows, x_ref.shape[0])

    def body(i, carry):
        start = pl.multiple_of(i * rows, rows)
        x = x_ref[pl.ds(start, rows), :]
        ms = jnp.mean(x * x, axis=-1, keepdims=True)
        h_ref[pl.ds(start + dst_offset, rows), :] = (x * lax.rsqrt(ms + EPS) * g).astype(h_ref.dtype)
        return carry

    lax.fori_loop(0, x_ref.shape[0] // rows, body, 0)


def _norm_mm_body(*refs, n_w, epilogue, normed):
    if normed:
        h_ref, refs = refs[0], refs[1:]
    else:
        x_ref, g_ref, h_ref, refs = refs[0], refs[1], refs[-1], refs[2:-1]

        @pl.when(pl.program_id(1) == 0)
        def _():
            _rmsnorm_rows(x_ref, g_ref, h_ref)

    w_refs, b_refs, o_ref = refs[:n_w], refs[n_w:2 * n_w], refs[2 * n_w]

    for r in range(o_ref.shape[0] // SUB_ROWS):
        rows = slice(r * SUB_ROWS, (r + 1) * SUB_ROWS)
        h = h_ref[rows, :]
        accs = [jnp.dot(h, w[...], preferred_element_type=F32) + b[...]
                for w, b in zip(w_refs, b_refs)]
        o_ref[rows, :] = epilogue(*accs).astype(o_ref.dtype)


def _norm_matmul(x, g, w, b, col_offsets, n_out, epilogue, out_dtype=F32):
    m, k = x.shape
    normed = x.dtype == BF16
    tm, tn = TILES["proj_normed" if normed else "proj"]
    n_w = len(col_offsets)
    col_map = lambda i, j, o: (0, 0, j + o)
    w_specs = [pl.BlockSpec((None, k, tn), functools.partial(col_map, o=off // tn)) for off in col_offsets]
    b_specs = [pl.BlockSpec((None, 1, tn), functools.partial(col_map, o=off // tn)) for off in col_offsets]
    b3 = b.reshape(1, 1, -1)
    return pl.pallas_call(
        functools.partial(_norm_mm_body, n_w=n_w, epilogue=epilogue, normed=normed),
        name="norm_matmul_" + epilogue.__name__.strip("_"),
        grid=(m // tm, n_out // tn),
        in_specs=[pl.BlockSpec((tm, k), lambda i, j: (i, 0))]
        + ([] if normed else [pl.BlockSpec((1, k), lambda i, j: (0, 0))]) + w_specs + b_specs,
        out_specs=pl.BlockSpec((tm, tn), lambda i, j: (i, j)),
        out_shape=jax.ShapeDtypeStruct((m, n_out), out_dtype),
        scratch_shapes=[] if normed else [pltpu.VMEM((tm, k), BF16)],
        compiler_params=_params("parallel", "arbitrary"),
    )(x, *([] if normed else [g.reshape(1, k)]), *([w] * n_w), *([b3] * n_w))


def _qk_proj_body(h_ref, hh_ref, w_ref, b_ref, cw_ref, cb_ref, wg_ref, bg_ref, o_ref, og_ref, p_ref,
                  *, seq_tiles, q_tiles, q_scale):
    i, j = pl.program_id(0), pl.program_id(1)
    tm, tn = o_ref.shape
    kw = cw_ref.shape[0]

    @pl.when(j == 0)
    def _():
        og_ref[...] = jnp.dot(h_ref[...], wg_ref[...], preferred_element_type=F32) + bg_ref[...]

    first = (i % seq_tiles) == 0
    scale = jnp.where(j < q_tiles, q_scale, 1.0)
    for r in range(tm // SUB_ROWS):
        lo = QK_HALO + r * SUB_ROWS
        lhs = h_ref[r * SUB_ROWS:(r + 1) * SUB_ROWS, :]
        if r == 0:
            lhs = jnp.concatenate([hh_ref[...], lhs], axis=0)
        acc = jnp.dot(lhs, w_ref[...], preferred_element_type=F32) + b_ref[...]
        for p in range(tn // LANES):
            lanes = slice(p * LANES, (p + 1) * LANES)
            if r == 0:
                p_ref[p, 0:QK_HALO, :] = jnp.where(first, 0.0, acc[0:QK_HALO, lanes])
            p_ref[p, lo:lo + SUB_ROWS, :] = acc[acc.shape[0] - SUB_ROWS:, lanes]
            y = cb_ref[:, lanes]
            for t in range(kw):
                start = lo - (kw - 1) + t
                y = y + cw_ref[t:t + 1, lanes] * p_ref[p, start:start + SUB_ROWS, :]
            o_ref[r * SUB_ROWS:(r + 1) * SUB_ROWS, lanes] = (y * jax.nn.sigmoid(y) * scale).astype(o_ref.dtype)


def _qk_projection(h, w, b, conv_w, conv_b, w_gate, b_gate, n_q, q_scale, seq):
    m, k = h.shape
    tm, tn = TILES["proj_normed"]
    n_out = conv_w.shape[1]
    kw = conv_w.shape[0]
    assert kw - 1 <= QK_HALO and seq % tm == 0 and tm % QK_HALO == 0
    halo_blocks = tm // QK_HALO
    b3 = b.reshape(1, 1, -1)
    return pl.pallas_call(
        functools.partial(_qk_proj_body, seq_tiles=seq // tm, q_tiles=n_q // tn, q_scale=q_scale),
        name="mlstm_qk_projection",
        grid=(m // tm, n_out // tn),
        in_specs=[pl.BlockSpec((tm, k), lambda i, j: (i, 0)),
                  pl.BlockSpec((QK_HALO, k), lambda i, j: (jnp.maximum(i * halo_blocks - 1, 0), 0)),
                  pl.BlockSpec((None, k, tn), lambda i, j: (0, 0, j)),
                  pl.BlockSpec((None, 1, tn), lambda i, j: (0, 0, j)),
                  pl.BlockSpec((kw, tn), lambda i, j: (0, j)),
                  pl.BlockSpec((1, tn), lambda i, j: (0, j)),
                  pl.BlockSpec((k, LANES), lambda i, j: (0, 0)),
                  pl.BlockSpec((1, LANES), lambda i, j: (0, 0))],
        out_specs=[pl.BlockSpec((tm, tn), lambda i, j: (i, j)),
                   pl.BlockSpec((tm, LANES), lambda i, j: (i, 0))],
        out_shape=[jax.ShapeDtypeStruct((m, n_out), BF16), jax.ShapeDtypeStruct((m, LANES), F32)],
        scratch_shapes=[pltpu.VMEM((tn // LANES, QK_HALO + tm, LANES), F32)],
        compiler_params=_params("parallel", "arbitrary"),
    )(h, h, w, b3, conv_w, conv_b.reshape(1, n_out), w_gate, b_gate.reshape(1, LANES))


def _mm_res_body(a_ref, w_ref, x_ref, o_ref):
    o_ref[...] = x_ref[...] + jnp.dot(a_ref[...], w_ref[...], preferred_element_type=F32)


def _matmul_residual(a, w, x):
    m, k = a.shape
    n = w.shape[2]
    tm, tn = TILES["out"]
    return pl.pallas_call(
        _mm_res_body,
        name="matmul_residual",
        grid=(m // tm, n // tn),
        in_specs=[pl.BlockSpec((tm, k), lambda i, j: (i, 0)),
                  pl.BlockSpec((None, k, tn), lambda i, j: (0, 0, j)),
                  pl.BlockSpec((tm, tn), lambda i, j: (i, j))],
        out_specs=pl.BlockSpec((tm, tn), lambda i, j: (i, j)),
        out_shape=jax.ShapeDtypeStruct((m, n), F32),
        compiler_params=_params("parallel", "arbitrary"),
    )(a, w, x)


def _ffn_body(*refs, n_cast, final_norm):
    x_hbm, g_ref, w1_ref, w2_ref, fg_ref = refs[:5]
    src_refs = refs[5:5 + n_cast]
    h_ref, x_buf, x_sem = refs[-3:]
    outs = refs[5 + n_cast:-3]
    o_ref, normed_ref = (outs[0], outs[0]) if final_norm else outs[:2]
    dst_refs = outs[len(outs) - n_cast:]
    i, j = pl.program_id(0), pl.program_id(1)
    tm = o_ref.shape[0]

    def x_copy(tile):
        return pltpu.make_async_copy(x_hbm.at[pl.ds(tile * tm, tm), :], x_buf, x_sem)

    @pl.when(j == 0)
    def _():
        @pl.when(i == 0)
        def _():
            x_copy(0).start()

        x_copy(i).wait()
        _rmsnorm_rows(x_buf, g_ref, h_ref)
        o_ref[...] = x_buf[...]

        @pl.when(i + 1 < pl.num_programs(0))
        def _():
            x_copy(i + 1).start()

    for r in range(tm // SUB_ROWS):
        rows = slice(r * SUB_ROWS, (r + 1) * SUB_ROWS)
        a = jnp.dot(h_ref[rows, :], w1_ref[...], preferred_element_type=F32)
        a = jnp.square(jnp.maximum(a, 0.0)).astype(BF16)
        o_ref[rows, :] += jnp.dot(a, w2_ref[...], preferred_element_type=F32)

    for src, dst in zip(src_refs, dst_refs):
        dst[...] = src[...].astype(BF16)

    @pl.when(j == pl.num_programs(1) - 1)
    def _():
        _rmsnorm_rows(o_ref, fg_ref, normed_ref)


def _ffn(x, g, w1, w2, next_g, final_norm, cast_jobs=()):
    m, d = x.shape
    f = w1.shape[2]
    tm, tf = TILES["ffn"]
    nj = f // tf
    steps = (m // tm) * nj
    src_specs, dst_specs, dst_shapes = [], [], []
    for stack, l in cast_jobs:
        _, r, c = stack.shape
        rb = max(BF16_ROWS, r // steps)
        hold = steps // (r // rb)
        assert r % rb == 0 and steps % (r // rb) == 0
        src_specs.append(pl.BlockSpec((None, rb, c), functools.partial(
            lambda i, j, l, hold: (l, (i * nj + j) // hold, 0), l=l, hold=hold)))
        dst_specs.append(pl.BlockSpec((None, rb, c), functools.partial(
            lambda i, j, hold: (0, (i * nj + j) // hold, 0), hold=hold)))
        dst_shapes.append(jax.ShapeDtypeStruct((1, r, c), BF16))
    normed_spec = [] if final_norm else [pl.BlockSpec((tm, d), lambda i, j: (i, 0))]
    normed_shape = [] if final_norm else [jax.ShapeDtypeStruct((m, d), BF16)]
    outs = pl.pallas_call(
        functools.partial(_ffn_body, n_cast=len(cast_jobs), final_norm=final_norm),
        name="ffn_final" if final_norm else "ffn",
        grid=(m // tm, nj),
        in_specs=[pl.BlockSpec(memory_space=pl.ANY),
                  pl.BlockSpec((1, d), lambda i, j: (0, 0)),
                  pl.BlockSpec((None, d, tf), lambda i, j: (0, 0, j)),
                  pl.BlockSpec((None, tf, d), lambda i, j: (0, j, 0)),
                  pl.BlockSpec((1, d), lambda i, j: (0, 0))] + src_specs,
        out_specs=[pl.BlockSpec((tm, d), lambda i, j: (i, 0))] + normed_spec + dst_specs,
        out_shape=[jax.ShapeDtypeStruct((m, d), F32)] + normed_shape + dst_shapes,
        scratch_shapes=[pltpu.VMEM((tm, d), BF16), pltpu.VMEM((tm, d), F32), pltpu.SemaphoreType.DMA(())],
        compiler_params=_params("arbitrary", "arbitrary"),
    )(x, g.reshape(1, d), w1, w2, next_g.reshape(1, d), *[stack for stack, _ in cast_jobs])
    if final_norm:
        return outs[0], None, list(outs[1:])
    return outs[0], outs[1], list(outs[2:])


def _conv_body(gc_ref, gp_ref, x_ref, wdw_ref, bdw_ref, lng_ref, lnb_ref, wout_ref, bout_ref,
               o_ref, ext_ref, y_ref, s_ref, *, seq_tiles, row_chunk):
    ts, d = gc_ref.shape
    kw = wdw_ref.shape[0]
    first = (pl.program_id(0) % seq_tiles) == 0
    lead = CONV_HALO - (kw - 1)

    for p in range(d // LANES):
        lanes = slice(p * LANES, (p + 1) * LANES)
        ext_ref[p, 0:CONV_HALO, :] = jnp.where(first, 0.0, gp_ref[:, lanes])
        ext_ref[p, CONV_HALO:, :] = gc_ref[:, lanes]
        w_rows = [wdw_ref[k:k + 1, lanes] for k in range(kw)]
        bias = jnp.broadcast_to(bdw_ref[:, lanes], (row_chunk, LANES))

        def row_body(rc, carry, p=p, lanes=lanes, w_rows=w_rows, bias=bias):
            base = pl.multiple_of(rc * row_chunk, row_chunk)
            acc = bias
            for k in range(kw):
                acc = acc + w_rows[k] * ext_ref[p, pl.ds(base + lead + k, row_chunk), :]
            y_ref[pl.ds(base, row_chunk), lanes] = acc
            return carry

        lax.fori_loop(0, ts // row_chunk, row_body, 0)

    lng = lng_ref[...]
    lnb = lnb_ref[...]
    rows = 256

    def ln_body(i, carry):
        r = pl.ds(pl.multiple_of(i * rows, rows), rows)
        y = y_ref[r, :]
        mu = jnp.mean(y, axis=-1, keepdims=True)
        yc = y - mu
        var = jnp.mean(yc * yc, axis=-1, keepdims=True)
        yn = yc * lax.rsqrt(var + EPS) * lng + lnb
        s_ref[r, :] = (yn * jax.nn.sigmoid(yn)).astype(BF16)
        return carry

    lax.fori_loop(0, ts // rows, ln_body, 0)
    o_ref[...] = (x_ref[...] + jnp.dot(s_ref[...], wout_ref[...], preferred_element_type=F32)
                  + bout_ref[...])


def _conv_tail(g, x, w_dw, b_dw, ln_g, ln_b, w_out, b_out, seq):
    m, d = g.shape
    kw = w_dw.shape[0]
    ts = TILES["conv_rows"]
    assert kw - 1 <= CONV_HALO and seq % ts == 0 and ts % CONV_HALO == 0
    halo_blocks = ts // CONV_HALO
    row = lambda v: v.reshape(1, d)
    const = lambda i: (0, 0)
    return pl.pallas_call(
        functools.partial(_conv_body, seq_tiles=seq // ts, row_chunk=64),
        name="conv_tail",
        grid=(m // ts,),
        in_specs=[pl.BlockSpec((ts, d), lambda i: (i, 0)),
                  pl.BlockSpec((CONV_HALO, d), lambda i: (jnp.maximum(i * halo_blocks - 1, 0), 0)),
                  pl.BlockSpec((ts, d), lambda i: (i, 0)),
                  pl.BlockSpec((kw, d), const),
                  pl.BlockSpec((1, d), const), pl.BlockSpec((1, d), const), pl.BlockSpec((1, d), const),
                  pl.BlockSpec((None, d, d), lambda i: (0, 0, 0)),
                  pl.BlockSpec((1, d), const)],
        out_specs=pl.BlockSpec((ts, d), lambda i: (i, 0)),
        out_shape=jax.ShapeDtypeStruct((m, d), F32),
        scratch_shapes=[pltpu.VMEM((d // LANES, ts + CONV_HALO, LANES), F32),
                        pltpu.VMEM((ts, d), F32),
                        pltpu.VMEM((ts, d), BF16)],
        compiler_params=_params("arbitrary"),
    )(g, g, x, w_dw, row(b_dw), row(ln_g), row(ln_b), w_out, row(b_out))


def _mlstm_body(q_ref, k_ref, v_ref, so_ref, gate_ref, ng_ref, y_ref, c_ref, n_ref, m_ref):
    L = q_ref.shape[0]
    n_heads = gate_ref.shape[0]
    dqk = q_ref.shape[1] // n_heads
    dv = v_ref.shape[1] // n_heads

    @pl.when(pl.program_id(2) == 0)
    def _():
        c_ref[...] = jnp.zeros(c_ref.shape, F32)
        n_ref[...] = jnp.zeros(n_ref.shape, F32)
        m_ref[...] = jnp.zeros(m_ref.shape, F32)

    t_idx = lax.broadcasted_iota(jnp.int32, (L, L), 0)
    s_idx = lax.broadcasted_iota(jnp.int32, (L, L), 1)
    causal = s_idx <= t_idx
    diag = s_idx == t_idx

    for hd in range(n_heads):
        qk_cols = slice(hd * dqk, (hd + 1) * dqk)
        v_cols = slice(hd * dv, (hd + 1) * dv)
        qb = q_ref[:, qk_cols]
        kb = k_ref[:, qk_cols]
        vb = v_ref[:, v_cols]

        ig_r = gate_ref[hd, 0:1, :]
        lf_r = jax.nn.log_sigmoid(gate_ref[hd, 1:2, :])
        lf_c = jnp.sum(jnp.where(diag, lf_r, 0.0), axis=1, keepdims=True)
        ig_c = jnp.sum(jnp.where(diag, ig_r, 0.0), axis=1, keepdims=True)
        b_c = jnp.sum(jnp.where(causal, lf_r, 0.0), axis=1, keepdims=True)
        b_r = jnp.sum(jnp.where(t_idx <= s_idx, lf_c, 0.0), axis=0, keepdims=True)
        a_r = ig_r - b_r
        a_c = ig_c - b_c
        a_mat = jnp.where(causal, a_r, -jnp.inf)
        m_prev = m_ref[hd, 0:1, 0:1]
        g_c = jnp.maximum(m_prev, jnp.max(a_mat, axis=1, keepdims=True))
        d_mat = jnp.exp(a_mat - g_c)
        inter_c = jnp.exp(m_prev - g_c)
        m_c = b_c + g_c
        b_last = b_c[L - 1:L, :]
        m_new = m_c[L - 1:L, :]

        s = lax.dot_general(qb, kb, (((1,), (1,)), ((), ())), preferred_element_type=F32) * d_mat
        num = (jnp.dot(s.astype(BF16), vb, preferred_element_type=F32)
               + inter_c * jnp.dot(qb, c_ref[hd].astype(BF16), preferred_element_type=F32))
        den = (jnp.sum(s, axis=1, keepdims=True)
               + inter_c * jnp.sum(qb.astype(F32) * n_ref[hd], axis=1, keepdims=True))
        h = num / jnp.maximum(jnp.abs(den), jnp.exp(-m_c))

        w_c = jnp.exp(b_last + a_c - m_new)
        decay = jnp.exp(b_last + m_prev - m_new)
        c_ref[hd] = decay * c_ref[hd] + lax.dot_general(
            kb, (w_c * vb.astype(F32)).astype(BF16), (((0,), (0,)), ((), ())), preferred_element_type=F32)
        n_ref[hd] = decay * n_ref[hd] + jnp.sum(w_c * kb.astype(F32), axis=0, keepdims=True)
        m_ref[hd] = jnp.broadcast_to(m_new, m_ref.shape[1:])

        hn = h * lax.rsqrt(jnp.mean(h * h, axis=1, keepdims=True) + EPS) * ng_ref[:, v_cols]
        y_ref[:, v_cols] = (so_ref[:, v_cols] * hn).astype(y_ref.dtype)


def _mlstm_recurrence(qk, v, sig_o, gates, norm_g, batch, seq):
    heads = MLSTM_HEADS
    hp = MLSTM_HEADS_PER_STEP
    L = MLSTM_CHUNK
    hqk = qk.shape[1] // 2
    hv = v.shape[1]
    dqk, dv = hqk // heads, hv // heads
    groups = heads // hp
    as3 = lambda t: t.reshape(batch, seq, t.shape[1])
    head_blk = lambda width: pl.BlockSpec((None, L, hp * width), lambda b, h, c: (b, c, h))
    return pl.pallas_call(
        _mlstm_body,
        name="mlstm_recurrence",
        grid=(batch, groups, seq // L),
        in_specs=[head_blk(dqk),
                  pl.BlockSpec((None, L, hp * dqk), lambda b, h, c: (b, c, groups + h)),
                  head_blk(dv),
                  head_blk(dv),
                  pl.BlockSpec((None, hp, None, 2, L), lambda b, h, c: (b, h, c, 0, 0)),
                  pl.BlockSpec((1, hp * dv), lambda b, h, c: (0, h))],
        out_specs=head_blk(dv),
        out_shape=jax.ShapeDtypeStruct((batch, seq, hv), BF16),
        scratch_shapes=[pltpu.VMEM((hp, dqk, dv), F32),
                        pltpu.VMEM((hp, 1, dqk), F32),
                        pltpu.VMEM((hp, 8, LANES), F32)],
        compiler_params=_params("parallel", "parallel", "arbitrary"),
    )(as3(qk), as3(qk), as3(v), as3(sig_o), gates, norm_g.reshape(1, hv)).reshape(batch * seq, hv)


def _sgu_body(u_ref, v_ref, x_ref, ng_ref, ws_ref, bst_ref, wout_ref, bout_ref, o_ref,
              vn_ref, gated_ref):
    tm, width = u_ref.shape
    groups, chunk, _ = ws_ref.shape
    gw = width // groups
    _rmsnorm_rows(v_ref, ng_ref, vn_ref)
    t_idx = lax.broadcasted_iota(jnp.int32, (chunk, chunk), 0)
    s_idx = lax.broadcasted_iota(jnp.int32, (chunk, chunk), 1)
    tril = (s_idx <= t_idx).astype(F32)
    for g in range(groups):
        ws_g = (ws_ref[g] * tril).astype(BF16)
        bias_c = bst_ref[:, g:g + 1]
        cols = slice(g * gw, (g + 1) * gw)
        for c in range(tm // chunk):
            rows = slice(c * chunk, (c + 1) * chunk)
            mixed = jnp.dot(ws_g, vn_ref[rows, cols], preferred_element_type=F32) + bias_c
            gated_ref[rows, cols] = (u_ref[rows, cols] * mixed).astype(BF16)
    o_ref[...] = (x_ref[...] + jnp.dot(gated_ref[...], wout_ref[...], preferred_element_type=F32)
                  + bout_ref[...])


def _sgu_tail(p, x, norm_g, w_s, b_s, w_out, b_out):
    m, d = x.shape
    width = norm_g.shape[0]
    groups, chunk, _ = w_s.shape
    tm = TILES["sgu_rows"]
    const2 = lambda i: (0, 0)
    return pl.pallas_call(
        _sgu_body,
        name="sgu_tail",
        grid=(m // tm,),
        in_specs=[pl.BlockSpec((tm, width), lambda i: (i, 0)),
                  pl.BlockSpec((tm, width), lambda i: (i, 1)),
                  pl.BlockSpec((tm, d), lambda i: (i, 0)),
                  pl.BlockSpec((1, width), const2),
                  pl.BlockSpec((groups, chunk, chunk), lambda i: (0, 0, 0)),
                  pl.BlockSpec((chunk, groups), const2),
                  pl.BlockSpec((None, width, d), lambda i: (0, 0, 0)),
                  pl.BlockSpec((1, d), const2)],
        out_specs=pl.BlockSpec((tm, d), lambda i: (i, 0)),
        out_shape=jax.ShapeDtypeStruct((m, d), F32),
        scratch_shapes=[pltpu.VMEM((tm, width), BF16), pltpu.VMEM((tm, width), BF16)],
        compiler_params=_params("parallel"),
    )(p, p, x, norm_g.reshape(1, width), w_s, jnp.transpose(b_s), w_out, b_out.reshape(1, d))


def _glu(val, gate):
    return val * jax.nn.sigmoid(gate)


def _identity(a):
    return a


def _sigmoid(a):
    return jax.nn.sigmoid(a)


def _gelu(a):
    return 0.5 * a * (1.0 + lax.erf(a * (2.0 ** -0.5)))


def _mlstm_layer(x, normed, w_in, b_in, w_qkconv, b_qkconv, norm_g, w_out, batch, seq):
    heads = MLSTM_HEADS
    n_qk = w_qkconv.shape[1]
    hv = norm_g.shape[0]
    dqk = n_qk // 2 // heads
    n_main = n_qk + 2 * hv
    w_gate = jnp.pad(w_in[0, :, n_main:], ((0, 0), (0, LANES - 2 * heads)))
    b_gate = jnp.pad(b_in[n_main:], (0, LANES - 2 * heads))
    qk, gates = _qk_projection(normed, w_in, b_in, w_qkconv, b_qkconv, w_gate, b_gate, n_qk // 2,
                               dqk ** -0.5, seq)
    v = _norm_matmul(normed, None, w_in, b_in, (n_qk,), hv, _identity, out_dtype=BF16)
    sig_o = _norm_matmul(normed, None, w_in, b_in, (n_qk + hv,), hv, _sigmoid)
    nc = seq // MLSTM_CHUNK
    gates = gates[:, :2 * heads].reshape(batch, nc, MLSTM_CHUNK, 2, heads)
    gates = jnp.transpose(gates, (0, 4, 1, 3, 2))
    y = _mlstm_recurrence(qk, v, sig_o, gates, norm_g, batch, seq)
    return _matmul_residual(y, w_out, x)


def kernel(x, norm_mix_g, norm_ffn_g, final_g, conv_w_in, conv_b_in, conv_w_dw, conv_b_dw, conv_ln_g, conv_ln_b, conv_w_out, conv_b_out, mlstm_w_in, mlstm_b_in, mlstm_w_qkconv, mlstm_b_qkconv, mlstm_norm_g, mlstm_w_out, sgu_w_in, sgu_b_in, sgu_norm_g, sgu_w_s, sgu_b_s, sgu_w_out, sgu_b_out, ffn_w1, ffn_w2):
    batch, seq, d = x.shape
    depth = norm_mix_g.shape[0]
    mixer_weights = ((conv_w_in, conv_w_out), (mlstm_w_in, mlstm_w_out), (sgu_w_in, sgu_w_out))

    def layer_weight_jobs(i):
        return [(w, i // N_MIXERS) for w in mixer_weights[i % N_MIXERS]] + [(ffn_w1, i), (ffn_w2, i)]

    w_mix_in, w_mix_out, w1, w2 = (w[l:l + 1].astype(BF16) for w, l in layer_weight_jobs(0))
    h = x.reshape(batch * seq, d)
    normed = None
    for i in range(depth):
        kind, j = i % N_MIXERS, i // N_MIXERS
        src, ng = (h, norm_mix_g[i]) if normed is None else (normed, None)
        if kind == 0:
            g = _norm_matmul(src, ng, w_mix_in, conv_b_in[j], (0, d), d, _glu)
            h = _conv_tail(g, h, conv_w_dw[j], conv_b_dw[j], conv_ln_g[j], conv_ln_b[j], w_mix_out,
                           conv_b_out[j], seq)
        elif kind == 1:
            assert normed is not None, "the mLSTM projections take the normalised stream of a preceding layer"
            h = _mlstm_layer(h, normed, w_mix_in, mlstm_b_in[j], mlstm_w_qkconv[j], mlstm_b_qkconv[j],
                             mlstm_norm_g[j], w_mix_out, batch, seq)
        else:
            p = _norm_matmul(src, ng, w_mix_in, sgu_b_in[j], (0,), sgu_w_in.shape[2], _gelu)
            h = _sgu_tail(p, h, sgu_norm_g[j], sgu_w_s[j], sgu_b_s[j], w_mix_out, sgu_b_out[j])
        last = i == depth - 1
        jobs = [] if last else layer_weight_jobs(i + 1)
        inside = [n for n, (w, _) in enumerate(jobs) if w.shape[2] % LANES == 0]
        h, normed, cast = _ffn(h, norm_ffn_g[i], w1, w2, final_g if last else norm_mix_g[i + 1],
                               final_norm=last, cast_jobs=[jobs[n] for n in inside])
        if not last:
            nxt = {n: c for n, c in zip(inside, cast)}
            w_mix_in, w_mix_out, w1, w2 = (nxt[n] if n in nxt else w[l:l + 1].astype(BF16)
                                           for n, (w, l) in enumerate(jobs))
    return h.reshape(batch, seq, d)
```

```python
import functools

import jax
import jax.numpy as jnp
from jax import lax
from jax.experimental import pallas as pl
from jax.experimental.pallas import tpu as pltpu

EPS = 1e-6
N_MIXERS = 3
MLSTM_HEADS = 8
MLSTM_CHUNK = 256
MLSTM_HEADS_PER_STEP = 2
SGU_CHUNK = 128
SGU_GROUPS = 8
CONV_HALO = 32
QK_HALO = 16
LANES = 128
BF16_ROWS = 16
SUB_ROWS = 256

F32 = jnp.float32
BF16 = jnp.bfloat16

V7X_VMEM_BYTES = 64 * 1024 * 1024
VMEM_LIMIT = V7X_VMEM_BYTES - 4 * 1024 * 1024

TILES = dict(
    ffn=(1024, 1024),
    proj=(1024, 1024),
    proj_normed=(2048, 1024),
    out=(1024, 512),
    conv_rows=512,
    sgu_rows=512,
)


def _params(*sem):
    return pltpu.CompilerParams(dimension_semantics=sem, vmem_limit_bytes=VMEM_LIMIT)


def _rmsnorm_rows(x_ref, g_ref, h_ref, dst_offset=0, rows=256):
    g = g_ref[...]
    rows = min(rows, x_ref.shape[0])

    def body(i, carry):
        start = pl.multiple_of(i * rows, rows)
        x = x_ref[pl.ds(start, rows), :]
        ms = jnp.mean(x * x, axis=-1, keepdims=True)
        h_ref[pl.ds(start + dst_offset, rows), :] = (x * lax.rsqrt(ms + EPS) * g).astype(h_ref.dtype)
        return carry

    lax.fori_loop(0, x_ref.shape[0] // rows, body, 0)


def _norm_mm_body(*refs, n_w, epilogue, normed):
    if normed:
        h_ref, refs = refs[0], refs[1:]
    else:
        x_ref, g_ref, h_ref, refs = refs[0], refs[1], refs[-1], refs[2:-1]

        @pl.when(pl.program_id(1) == 0)
        def _():
            _rmsnorm_rows(x_ref, g_ref, h_ref)

    w_refs, b_refs, o_ref = refs[:n_w], refs[n_w:2 * n_w], refs[2 * n_w]

    for r in range(o_ref.shape[0] // SUB_ROWS):
        rows = slice(r * SUB_ROWS, (r + 1) * SUB_ROWS)
        h = h_ref[rows, :]
        accs = [jnp.dot(h, w[...], preferred_element_type=F32) + b[...]
                for w, b in zip(w_refs, b_refs)]
        o_ref[rows, :] = epilogue(*accs).astype(o_ref.dtype)


def _norm_matmul(x, g, w, b, col_offsets, n_out, epilogue, out_dtype=F32):
    m, k = x.shape
    normed = x.dtype == BF16
    tm, tn = TILES["proj_normed" if normed else "proj"]
    n_w = len(col_offsets)
    col_map = lambda i, j, o: (0, 0, j + o)
    w_specs = [pl.BlockSpec((None, k, tn), functools.partial(col_map, o=off // tn)) for off in col_offsets]
    b_specs = [pl.BlockSpec((None, 1, tn), functools.partial(col_map, o=off // tn)) for off in col_offsets]
    b3 = b.reshape(1, 1, -1)
    return pl.pallas_call(
        functools.partial(_norm_mm_body, n_w=n_w, epilogue=epilogue, normed=normed),
        name="norm_matmul_" + epilogue.__name__.strip("_"),
        grid=(m // tm, n_out // tn),
        in_specs=[pl.BlockSpec((tm, k), lambda i, j: (i, 0))]
        + ([] if normed else [pl.BlockSpec((1, k), lambda i, j: (0, 0))]) + w_specs + b_specs,
        out_specs=pl.BlockSpec((tm, tn), lambda i, j: (i, j)),
        out_shape=jax.ShapeDtypeStruct((m, n_out), out_dtype),
        scratch_shapes=[] if normed else [pltpu.VMEM((tm, k), BF16)],
        compiler_params=_params("parallel", "arbitrary"),
    )(x, *([] if normed else [g.reshape(1, k)]), *([w] * n_w), *([b3] * n_w))


def _qk_proj_body(h_ref, hh_ref, w_ref, b_ref, cw_ref, cb_ref, wg_ref, bg_ref, o_ref, og_ref, p_ref,
                  *, seq_tiles, q_tiles, q_scale):
    i, j = pl.program_id(0), pl.program_id(1)
    tm, tn = o_ref.shape
    kw = cw_ref.shape[0]

    @pl.when(j == 0)
    def _():
        og_ref[...] = jnp.dot(h_ref[...], wg_ref[...], preferred_element_type=F32) + bg_ref[...]

    first = (i % seq_tiles) == 0
    scale = jnp.where(j < q_tiles, q_scale, 1.0)
    for r in range(tm // SUB_ROWS):
        lo = QK_HALO + r * SUB_ROWS
        lhs = h_ref[r * SUB_ROWS:(r + 1) * SUB_ROWS, :]
        if r == 0:
            lhs = jnp.concatenate([hh_ref[...], lhs], axis=0)
        acc = jnp.dot(lhs, w_ref[...], preferred_element_type=F32) + b_ref[...]
        for p in range(tn // LANES):
            lanes = slice(p * LANES, (p + 1) * LANES)
            if r == 0:
                p_ref[p, 0:QK_HALO, :] = jnp.where(first, 0.0, acc[0:QK_HALO, lanes])
            p_ref[p, lo:lo + SUB_ROWS, :] = acc[acc.shape[0] - SUB_ROWS:, lanes]
            y = cb_ref[:, lanes]
            for t in range(kw):
                start = lo - (kw - 1) + t
                y = y + cw_ref[t:t + 1, lanes] * p_ref[p, start:start + SUB_ROWS, :]
            o_ref[r * SUB_ROWS:(r + 1) * SUB_ROWS, lanes] = (y * jax.nn.sigmoid(y) * scale).astype(o_ref.dtype)


def _qk_projection(h, w, b, conv_w, conv_b, w_gate, b_gate, n_q, q_scale, seq):
    m, k = h.shape
    tm, tn = TILES["proj_normed"]
    n_out = conv_w.shape[1]
    kw = conv_w.shape[0]
    assert kw - 1 <= QK_HALO and seq % tm == 0 and tm % QK_HALO == 0
    halo_blocks = tm // QK_HALO
    b3 = b.reshape(1, 1, -1)
    return pl.pallas_call(
        functools.partial(_qk_proj_body, seq_tiles=seq // tm, q_tiles=n_q // tn, q_scale=q_scale),
        name="mlstm_qk_projection",
        grid=(m // tm, n_out // tn),
        in_specs=[pl.BlockSpec((tm, k), lambda i, j: (i, 0)),
                  pl.BlockSpec((QK_HALO, k), lambda i, j: (jnp.maximum(i * halo_blocks - 1, 0), 0)),
                  pl.BlockSpec((None, k, tn), lambda i, j: (0, 0, j)),
                  pl.BlockSpec((None, 1, tn), lambda i, j: (0, 0, j)),
                  pl.BlockSpec((kw, tn), lambda i, j: (0, j)),
                  pl.BlockSpec((1, tn), lambda i, j: (0, j)),
                  pl.BlockSpec((k, LANES), lambda i, j: (0, 0)),
                  pl.BlockSpec((1, LANES), lambda i, j: (0, 0))],
        out_specs=[pl.BlockSpec((tm, tn), lambda i, j: (i, j)),
                   pl.BlockSpec((tm, LANES), lambda i, j: (i, 0))],
        out_shape=[jax.ShapeDtypeStruct((m, n_out), BF16), jax.ShapeDtypeStruct((m, LANES), F32)],
        scratch_shapes=[pltpu.VMEM((tn // LANES, QK_HALO + tm, LANES), F32)],
        compiler_params=_params("parallel", "arbitrary"),
    )(h, h, w, b3, conv_w, conv_b.reshape(1, n_out), w_gate, b_gate.reshape(1, LANES))


def _mm_res_body(a_ref, w_ref, x_ref, o_ref):
    o_ref[...] = x_ref[...] + jnp.dot(a_ref[...], w_ref[...], preferred_element_type=F32)


def _matmul_residual(a, w, x):
    m, k = a.shape
    n = w.shape[2]
    tm, tn = TILES["out"]
    return pl.pallas_call(
        _mm_res_body,
        name="matmul_residual",
        grid=(m // tm, n // tn),
        in_specs=[pl.BlockSpec((tm, k), lambda i, j: (i, 0)),
                  pl.BlockSpec((None, k, tn), lambda i, j: (0, 0, j)),
                  pl.BlockSpec((tm, tn), lambda i, j: (i, j))],
        out_specs=pl.BlockSpec((tm, tn), lambda i, j: (i, j)),
        out_shape=jax.ShapeDtypeStruct((m, n), F32),
        compiler_params=_params("parallel", "arbitrary"),
    )(a, w, x)


def _ffn_body(*refs, n_cast, final_norm):
    x_hbm, g_ref, w1_ref, w2_ref, fg_ref = refs[:5]
    src_refs = refs[5:5 + n_cast]
    h_ref, x_buf, x_sem = refs[-3:]
    outs = refs[5 + n_cast:-3]
    o_ref, normed_ref = (outs[0], outs[0]) if final_norm else outs[:2]
    dst_refs = outs[len(outs) - n_cast:]
    i, j = pl.program_id(0), pl.program_id(1)
    tm = o_ref.shape[0]

    def x_copy(tile):
        return pltpu.make_async_copy(x_hbm.at[pl.ds(tile * tm, tm), :], x_buf, x_sem)

    @pl.when(j == 0)
    def _():
        @pl.when(i == 0)
        def _():
            x_copy(0).start()

        x_copy(i).wait()
        _rmsnorm_rows(x_buf, g_ref, h_ref)
        o_ref[...] = x_buf[...]

        @pl.when(i + 1 < pl.num_programs(0))
        def _():
            x_copy(i + 1).start()

    for r in range(tm // SUB_ROWS):
        rows = slice(r * SUB_ROWS, (r + 1) * SUB_ROWS)
        a = jnp.dot(h_ref[rows, :], w1_ref[...], preferred_element_type=F32)
        a = jnp.square(jnp.maximum(a, 0.0)).astype(BF16)
        o_ref[rows, :] += jnp.dot(a, w2_ref[...], preferred_element_type=F32)

    for src, dst in zip(src_refs, dst_refs):
        dst[...] = src[...].astype(BF16)

    @pl.when(j == pl.num_programs(1) - 1)
    def _():
        _rmsnorm_rows(o_ref, fg_ref, normed_ref)


def _ffn(x, g, w1, w2, next_g, final_norm, cast_jobs=()):
    m, d = x.shape
    f = w1.shape[2]
    tm, tf = TILES["ffn"]
    nj = f // tf
    steps = (m // tm) * nj
    src_specs, dst_specs, dst_shapes = [], [], []
    for stack, l in cast_jobs:
        _, r, c = stack.shape
        rb = max(BF16_ROWS, r // steps)
        hold = steps // (r // rb)
        assert r % rb == 0 and steps % (r // rb) == 0
        src_specs.append(pl.BlockSpec((None, rb, c), functools.partial(
            lambda i, j, l, hold: (l, (i * nj + j) // hold, 0), l=l, hold=hold)))
        dst_specs.append(pl.BlockSpec((None, rb, c), functools.partial(
            lambda i, j, hold: (0, (i * nj + j) // hold, 0), hold=hold)))
        dst_shapes.append(jax.ShapeDtypeStruct((1, r, c), BF16))
    normed_spec = [] if final_norm else [pl.BlockSpec((tm, d), lambda i, j: (i, 0))]
    normed_shape = [] if final_norm else [jax.ShapeDtypeStruct((m, d), BF16)]
    outs = pl.pallas_call(
        functools.partial(_ffn_body, n_cast=len(cast_jobs), final_norm=final_norm),
        name="ffn_final" if final_norm else "ffn",
        grid=(m // tm, nj),
        in_specs=[pl.BlockSpec(memory_space=pl.ANY),
                  pl.BlockSpec((1, d), lambda i, j: (0, 0)),
                  pl.BlockSpec((None, d, tf), lambda i, j: (0, 0, j)),
                  pl.BlockSpec((None, tf, d), lambda i, j: (0, j, 0)),
                  pl.BlockSpec((1, d), lambda i, j: (0, 0))] + src_specs,
        out_specs=[pl.BlockSpec((tm, d), lambda i, j: (i, 0))] + normed_spec + dst_specs,
        out_shape=[jax.ShapeDtypeStruct((m, d), F32)] + normed_shape + dst_shapes,
        scratch_shapes=[pltpu.VMEM((tm, d), BF16), pltpu.VMEM((tm, d), F32), pltpu.SemaphoreType.DMA(())],
        compiler_params=_params("arbitrary", "arbitrary"),
    )(x, g.reshape(1, d), w1, w2, next_g.reshape(1, d), *[stack for stack, _ in cast_jobs])
    if final_norm:
        return outs[0], None, list(outs[1:])
    return outs[0], outs[1], list(outs[2:])


def _conv_body(gc_ref, gp_ref, x_ref, wdw_ref, bdw_ref, lng_ref, lnb_ref, wout_ref, bout_ref,
               o_ref, ext_ref, y_ref, s_ref, *, seq_tiles, row_chunk):
    ts, d = gc_ref.shape
    kw = wdw_ref.shape[0]
    first = (pl.program_id(0) % seq_tiles) == 0
    lead = CONV_HALO - (kw - 1)

    for p in range(d // LANES):
        lanes = slice(p * LANES, (p + 1) * LANES)
        ext_ref[p, 0:CONV_HALO, :] = jnp.where(first, 0.0, gp_ref[:, lanes])
        ext_ref[p, CONV_HALO:, :] = gc_ref[:, lanes]
        w_rows = [wdw_ref[k:k + 1, lanes] for k in range(kw)]
        bias = jnp.broadcast_to(bdw_ref[:, lanes], (row_chunk, LANES))

        def row_body(rc, carry, p=p, lanes=lanes, w_rows=w_rows, bias=bias):
            base = pl.multiple_of(rc * row_chunk, row_chunk)
            acc = bias
            for k in range(kw):
                acc = acc + w_rows[k] * ext_ref[p, pl.ds(base + lead + k, row_chunk), :]
            y_ref[pl.ds(base, row_chunk), lanes] = acc
            return carry

        lax.fori_loop(0, ts // row_chunk, row_body, 0)

    lng = lng_ref[...]
    lnb = lnb_ref[...]
    rows = 256

    def ln_body(i, carry):
        r = pl.ds(pl.multiple_of(i * rows, rows), rows)
        y = y_ref[r, :]
        mu = jnp.mean(y, axis=-1, keepdims=True)
        yc = y - mu
        var = jnp.mean(yc * yc, axis=-1, keepdims=True)
        yn = yc * lax.rsqrt(var + EPS) * lng + lnb
        s_ref[r, :] = (yn * jax.nn.sigmoid(yn)).astype(BF16)
        return carry

    lax.fori_loop(0, ts // rows, ln_body, 0)
    o_ref[...] = (x_ref[...] + jnp.dot(s_ref[...], wout_ref[...], preferred_element_type=F32)
                  + bout_ref[...])


def _conv_tail(g, x, w_dw, b_dw, ln_g, ln_b, w_out, b_out, seq):
    m, d = g.shape
    kw = w_dw.shape[0]
    ts = TILES["conv_rows"]
    assert kw - 1 <= CONV_HALO and seq % ts == 0 and ts % CONV_HALO == 0
    halo_blocks = ts // CONV_HALO
    row = lambda v: v.reshape(1, d)
    const = lambda i: (0, 0)
    return pl.pallas_call(
        functools.partial(_conv_body, seq_tiles=seq // ts, row_chunk=128),
        name="conv_tail",
        grid=(m // ts,),
        in_specs=[pl.BlockSpec((ts, d), lambda i: (i, 0)),
                  pl.BlockSpec((CONV_HALO, d), lambda i: (jnp.maximum(i * halo_blocks - 1, 0), 0)),
                  pl.BlockSpec((ts, d), lambda i: (i, 0)),
                  pl.BlockSpec((kw, d), const),
                  pl.BlockSpec((1, d), const), pl.BlockSpec((1, d), const), pl.BlockSpec((1, d), const),
                  pl.BlockSpec((None, d, d), lambda i: (0, 0, 0)),
                  pl.BlockSpec((1, d), const)],
        out_specs=pl.BlockSpec((ts, d), lambda i: (i, 0)),
        out_shape=jax.ShapeDtypeStruct((m, d), F32),
        scratch_shapes=[pltpu.VMEM((d // LANES, ts + CONV_HALO, LANES), F32),
                        pltpu.VMEM((ts, d), F32),
                        pltpu.VMEM((ts, d), BF16)],
        compiler_params=_params("arbitrary"),
    )(g, g, x, w_dw, row(b_dw), row(ln_g), row(ln_b), w_out, row(b_out))


def _mlstm_body(q_ref, k_ref, v_ref, so_ref, gate_ref, ng_ref, y_ref, c_ref, n_ref, m_ref):
    L = q_ref.shape[0]
    n_heads = gate_ref.shape[0]
    dqk = q_ref.shape[1] // n_heads
    dv = v_ref.shape[1] // n_heads

    @pl.when(pl.program_id(2) == 0)
    def _():
        c_ref[...] = jnp.zeros(c_ref.shape, F32)
        n_ref[...] = jnp.zeros(n_ref.shape, F32)
        m_ref[...] = jnp.zeros(m_ref.shape, F32)

    t_idx = lax.broadcasted_iota(jnp.int32, (L, L), 0)
    s_idx = lax.broadcasted_iota(jnp.int32, (L, L), 1)
    causal = s_idx <= t_idx
    diag = s_idx == t_idx

    for hd in range(n_heads):
        qk_cols = slice(hd * dqk, (hd + 1) * dqk)
        v_cols = slice(hd * dv, (hd + 1) * dv)
        qb = q_ref[:, qk_cols]
        kb = k_ref[:, qk_cols]
        vb = v_ref[:, v_cols]

        ig_r = gate_ref[hd, 0:1, :]
        lf_r = jax.nn.log_sigmoid(gate_ref[hd, 1:2, :])
        lf_c = jnp.sum(jnp.where(diag, lf_r, 0.0), axis=1, keepdims=True)
        ig_c = jnp.sum(jnp.where(diag, ig_r, 0.0), axis=1, keepdims=True)
        b_c = jnp.sum(jnp.where(causal, lf_r, 0.0), axis=1, keepdims=True)
        b_r = jnp.sum(jnp.where(t_idx <= s_idx, lf_c, 0.0), axis=0, keepdims=True)
        a_r = ig_r - b_r
        a_c = ig_c - b_c
        a_mat = jnp.where(causal, a_r, -jnp.inf)
        m_prev = m_ref[hd, 0:1, 0:1]
        g_c = jnp.maximum(m_prev, jnp.max(a_mat, axis=1, keepdims=True))
        d_mat = jnp.exp(a_mat - g_c)
        inter_c = jnp.exp(m_prev - g_c)
        m_c = b_c + g_c
        b_last = b_c[L - 1:L, :]
        m_new = m_c[L - 1:L, :]

        s = lax.dot_general(qb, kb, (((1,), (1,)), ((), ())), preferred_element_type=F32) * d_mat
        num = (jnp.dot(s.astype(BF16), vb, preferred_element_type=F32)
               + inter_c * jnp.dot(qb, c_ref[hd].astype(BF16), preferred_element_type=F32))
        den = (jnp.sum(s, axis=1, keepdims=True)
               + inter_c * jnp.sum(qb.astype(F32) * n_ref[hd], axis=1, keepdims=True))
        h = num / jnp.maximum(jnp.abs(den), jnp.exp(-m_c))

        w_c = jnp.exp(b_last + a_c - m_new)
        decay = jnp.exp(b_last + m_prev - m_new)
        c_ref[hd] = decay * c_ref[hd] + lax.dot_general(
            kb, (w_c * vb.astype(F32)).astype(BF16), (((0,), (0,)), ((), ())), preferred_element_type=F32)
        n_ref[hd] = decay * n_ref[hd] + jnp.sum(w_c * kb.astype(F32), axis=0, keepdims=True)
        m_ref[hd] = jnp.broadcast_to(m_new, m_ref.shape[1:])

        hn = h * lax.rsqrt(jnp.mean(h * h, axis=1, keepdims=True) + EPS) * ng_ref[:, v_cols]
        y_ref[:, v_cols] = (so_ref[:, v_cols] * hn).astype(y_ref.dtype)


def _mlstm_recurrence(qk, v, sig_o, gates, norm_g, batch, seq):
    heads = MLSTM_HEADS
    hp = MLSTM_HEADS_PER_STEP
    L = MLSTM_CHUNK
    hqk = qk.shape[1] // 2
    hv = v.shape[1]
    dqk, dv = hqk // heads, hv // heads
    groups = heads // hp
    as3 = lambda t: t.reshape(batch, seq, t.shape[1])
    head_blk = lambda width: pl.BlockSpec((None, L, hp * width), lambda b, h, c: (b, c, h))
    return pl.pallas_call(
        _mlstm_body,
        name="mlstm_recurrence",
        grid=(batch, groups, seq // L),
        in_specs=[head_blk(dqk),
                  pl.BlockSpec((None, L, hp * dqk), lambda b, h, c: (b, c, groups + h)),
                  head_blk(dv),
                  head_blk(dv),
                  pl.BlockSpec((None, hp, None, 2, L), lambda b, h, c: (b, h, c, 0, 0)),
                  pl.BlockSpec((1, hp * dv), lambda b, h, c: (0, h))],
        out_specs=head_blk(dv),
        out_shape=jax.ShapeDtypeStruct((batch, seq, hv), BF16),
        scratch_shapes=[pltpu.VMEM((hp, dqk, dv), F32),
                        pltpu.VMEM((hp, 1, dqk), F32),
                        pltpu.VMEM((hp, 8, LANES), F32)],
        compiler_params=_params("parallel", "parallel", "arbitrary"),
    )(as3(qk), as3(qk), as3(v), as3(sig_o), gates, norm_g.reshape(1, hv)).reshape(batch * seq, hv)


def _sgu_body(u_ref, v_ref, x_ref, ng_ref, ws_ref, bst_ref, wout_ref, bout_ref, o_ref,
              vn_ref, gated_ref):
    tm, width = u_ref.shape
    groups, chunk, _ = ws_ref.shape
    gw = width // groups
    _rmsnorm_rows(v_ref, ng_ref, vn_ref)
    t_idx = lax.broadcasted_iota(jnp.int32, (chunk, chunk), 0)
    s_idx = lax.broadcasted_iota(jnp.int32, (chunk, chunk), 1)
    tril = (s_idx <= t_idx).astype(F32)
    for g in range(groups):
        ws_g = (ws_ref[g] * tril).astype(BF16)
        bias_c = bst_ref[:, g:g + 1]
        cols = slice(g * gw, (g + 1) * gw)
        for c in range(tm // chunk):
            rows = slice(c * chunk, (c + 1) * chunk)
            mixed = jnp.dot(ws_g, vn_ref[rows, cols], preferred_element_type=F32) + bias_c
            gated_ref[rows, cols] = (u_ref[rows, cols] * mixed).astype(BF16)
    o_ref[...] = (x_ref[...] + jnp.dot(gated_ref[...], wout_ref[...], preferred_element_type=F32)
                  + bout_ref[...])


def _sgu_tail(p, x, norm_g, w_s, b_s, w_out, b_out):
    m, d = x.shape
    width = norm_g.shape[0]
    groups, chunk, _ = w_s.shape
    tm = TILES["sgu_rows"]
    const2 = lambda i: (0, 0)
    return pl.pallas_call(
        _sgu_body,
        name="sgu_tail",
        grid=(m // tm,),
        in_specs=[pl.BlockSpec((tm, width), lambda i: (i, 0)),
                  pl.BlockSpec((tm, width), lambda i: (i, 1)),
                  pl.BlockSpec((tm, d), lambda i: (i, 0)),
                  pl.BlockSpec((1, width), const2),
                  pl.BlockSpec((groups, chunk, chunk), lambda i: (0, 0, 0)),
                  pl.BlockSpec((chunk, groups), const2),
                  pl.BlockSpec((None, width, d), lambda i: (0, 0, 0)),
                  pl.BlockSpec((1, d), const2)],
        out_specs=pl.BlockSpec((tm, d), lambda i: (i, 0)),
        out_shape=jax.ShapeDtypeStruct((m, d), F32),
        scratch_shapes=[pltpu.VMEM((tm, width), BF16), pltpu.VMEM((tm, width), BF16)],
        compiler_params=_params("parallel"),
    )(p, p, x, norm_g.reshape(1, width), w_s, jnp.transpose(b_s), w_out, b_out.reshape(1, d))


def _glu(val, gate):
    return val * jax.nn.sigmoid(gate)


def _identity(a):
    return a


def _sigmoid(a):
    return jax.nn.sigmoid(a)


def _gelu(a):
    return 0.5 * a * (1.0 + lax.erf(a * (2.0 ** -0.5)))


def _mlstm_layer(x, normed, w_in, b_in, w_qkconv, b_qkconv, norm_g, w_out, batch, seq):
    heads = MLSTM_HEADS
    n_qk = w_qkconv.shape[1]
    hv = norm_g.shape[0]
    dqk = n_qk // 2 // heads
    n_main = n_qk + 2 * hv
    w_gate = jnp.pad(w_in[0, :, n_main:], ((0, 0), (0, LANES - 2 * heads)))
    b_gate = jnp.pad(b_in[n_main:], (0, LANES - 2 * heads))
    qk, gates = _qk_projection(normed, w_in, b_in, w_qkconv, b_qkconv, w_gate, b_gate, n_qk // 2,
                               dqk ** -0.5, seq)
    v = _norm_matmul(normed, None, w_in, b_in, (n_qk,), hv, _identity, out_dtype=BF16)
    sig_o = _norm_matmul(normed, None, w_in, b_in, (n_qk + hv,), hv, _sigmoid)
    nc = seq // MLSTM_CHUNK
    gates = gates[:, :2 * heads].reshape(batch, nc, MLSTM_CHUNK, 2, heads)
    gates = jnp.transpose(gates, (0, 4, 1, 3, 2))
    y = _mlstm_recurrence(qk, v, sig_o, gates, norm_g, batch, seq)
    return _matmul_residual(y, w_out, x)


def kernel(x, norm_mix_g, norm_ffn_g, final_g, conv_w_in, conv_b_in, conv_w_dw, conv_b_dw, conv_ln_g, conv_ln_b, conv_w_out, conv_b_out, mlstm_w_in, mlstm_b_in, mlstm_w_qkconv, mlstm_b_qkconv, mlstm_norm_g, mlstm_w_out, sgu_w_in, sgu_b_in, sgu_norm_g, sgu_w_s, sgu_b_s, sgu_w_out, sgu_b_out, ffn_w1, ffn_w2):
    batch, seq, d = x.shape
    depth = norm_mix_g.shape[0]
    mixer_weights = ((conv_w_in, conv_w_out), (mlstm_w_in, mlstm_w_out), (sgu_w_in, sgu_w_out))

    def layer_weight_jobs(i):
        return [(w, i // N_MIXERS) for w in mixer_weights[i % N_MIXERS]] + [(ffn_w1, i), (ffn_w2, i)]

    w_mix_in, w_mix_out, w1, w2 = (w[l:l + 1].astype(BF16) for w, l in layer_weight_jobs(0))
    h = x.reshape(batch * seq, d)
    normed = None
    for i in range(depth):
        kind, j = i % N_MIXERS, i // N_MIXERS
        src, ng = (h, norm_mix_g[i]) if normed is None else (normed, None)
        if kind == 0:
            g = _norm_matmul(src, ng, w_mix_in, conv_b_in[j], (0, d), d, _glu)
            h = _conv_tail(g, h, conv_w_dw[j], conv_b_dw[j], conv_ln_g[j], conv_ln_b[j], w_mix_out,
                           conv_b_out[j], seq)
        elif kind == 1:
            assert normed is not None, "the mLSTM projections take the normalised stream of a preceding layer"
            h = _mlstm_layer(h, normed, w_mix_in, mlstm_b_in[j], mlstm_w_qkconv[j], mlstm_b_qkconv[j],
                             mlstm_norm_g[j], w_mix_out, batch, seq)
        else:
            p = _norm_matmul(src, ng, w_mix_in, sgu_b_in[j], (0,), sgu_w_in.shape[2], _gelu)
            h = _sgu_tail(p, h, sgu_norm_g[j], sgu_w_s[j], sgu_b_s[j], w_mix_out, sgu_b_out[j])
        last = i == depth - 1
        jobs = [] if last else layer_weight_jobs(i + 1)
        inside = [n for n, (w, _) in enumerate(jobs) if w.shape[2] % LANES == 0]
        h, normed, cast = _ffn(h, norm_ffn_g[i], w1, w2, final_g if last else norm_mix_g[i + 1],
                               final_norm=last, cast_jobs=[jobs[n] for n in inside])
        if not last:
            nxt = {n: c for n, c in zip(inside, cast)}
            w_mix_in, w_mix_out, w1, w2 = (nxt[n] if n in nxt else w[l:l + 1].astype(BF16)
                                           for n, (w, l) in enumerate(jobs))
    return h.reshape(batch, seq, d)
```
